```python
import jax, jax.numpy as jnp
from jax import lax
import numpy as np


D_MODEL = 1024
BATCH = 16
SEQ = 2048
DEPTH = 2

A_HEADS = 4
A_KEY_DIM = 128
A_VAL_DIM = 128
A_KEY_WIDTH = A_HEADS * A_KEY_DIM
A_WIDTH = A_HEADS * A_VAL_DIM
HGRN_CHUNK = 64
B_WIDTH = D_MODEL - A_WIDTH
B_BLOCKS = 8
B_BLOCK_DIM = B_WIDTH // B_BLOCKS
B_CONV = 4
RG_C = 8.0
C_HEADS = 16
C_HEAD_DIM = D_MODEL // C_HEADS
Q_BLOCK = 128
D_FF = 2816
FFN_CONV = 3
EPS = 1e-6

N_EVEN = (DEPTH + 1) // 2
N_ODD = DEPTH // 2
EVEN_SIZES = (A_KEY_WIDTH, A_KEY_WIDTH, A_WIDTH, A_WIDTH, B_WIDTH, B_WIDTH)
EVEN_IN = sum(EVEN_SIZES)
ODD_SIZES = (D_MODEL, D_MODEL, D_MODEL, C_HEADS)
ODD_IN = sum(ODD_SIZES)

kernel_name = 'hybrid_hgrn2_rglru_fox_convffn'


def rms_norm(x, gain):
    xf = x.astype(jnp.float32)
    y = xf * lax.rsqrt(jnp.mean(xf * xf, axis=-1, keepdims=True) + EPS)
    return (y * gain.astype(jnp.float32)).astype(x.dtype)


def split_cols(t, sizes):
    return jnp.split(t, list(np.cumsum(sizes)[:-1]), axis=-1)


def causal_dwconv(x, w, b):
    K, C = w.shape
    y = lax.conv_general_dilated(x, w[:, None, :].astype(x.dtype), window_strides=(1,),
                                 padding=[(K - 1, 0)], dimension_numbers=('NWC', 'WIO', 'NWC'),
                                 feature_group_count=C)
    return y + b.astype(x.dtype)


def hgrn2_mix(q, f_logit, v, g, lb, norm_gain):
    Bsz, T, _ = q.shape
    N = T // HGRN_CHUNK

    def heads(t, d):
        return t.reshape(Bsz, N, HGRN_CHUNK, A_HEADS, d).transpose(0, 3, 1, 2, 4)

    forget = lb + (1.0 - lb) * jax.nn.sigmoid(f_logit.astype(jnp.float32))
    qh = heads(jax.nn.silu(q.astype(jnp.float32)), A_KEY_DIM)
    kh = heads(1.0 - forget, A_KEY_DIM)
    bcum = jnp.cumsum(heads(jnp.log(forget), A_KEY_DIM), axis=3)
    vh = heads(v.astype(jnp.float32), A_VAL_DIM)
    b_last = bcum[:, :, :, -1:]
    q_dec = qh * jnp.exp(bcum)
    scores = jnp.einsum('bhnck,bhnsk->bhncs', q_dec, kh * jnp.exp(-bcum))
    causal = jnp.tril(jnp.ones((HGRN_CHUNK, HGRN_CHUNK), dtype=bool))
    scores = jnp.where(causal, scores, 0.0)
    o_intra = jnp.einsum('bhncs,bhnsv->bhncv', scores, vh)
    chunk_upd = jnp.einsum('bhnck,bhncv->bhnkv', kh * jnp.exp(b_last - bcum), vh)
    chunk_dec = jnp.exp(b_last[:, :, :, 0])

    def step(S, inp):
        dec, upd = inp
        return dec[..., None] * S + upd, S

    S0 = jnp.zeros((Bsz, A_HEADS, A_KEY_DIM, A_VAL_DIM), jnp.float32)
    _, S_prev = lax.scan(step, S0, (jnp.moveaxis(chunk_dec, 2, 0), jnp.moveaxis(chunk_upd, 2, 0)))
    S_prev = jnp.moveaxis(S_prev, 0, 2)
    o = o_intra + jnp.einsum('bhnck,bhnkv->bhncv', q_dec, S_prev)
    o = o.transpose(0, 2, 3, 1, 4).reshape(Bsz, T, A_HEADS, A_VAL_DIM)
    o = o * lax.rsqrt(jnp.mean(o * o, axis=-1, keepdims=True) + EPS)
    o = o * norm_gain.astype(jnp.float32).reshape(A_HEADS, A_VAL_DIM)
    return o.reshape(Bsz, T, A_WIDTH) * jax.nn.sigmoid(g.astype(jnp.float32))


def block_diag(x, w, b):
    xb = x.reshape(x.shape[0], x.shape[1], B_BLOCKS, B_BLOCK_DIM)
    return jnp.einsum('btni,nij->btnj', xb, w.astype(jnp.float32)).reshape(x.shape) + b.astype(jnp.float32)


def rglru_mix(x_br, y_br, conv_w, conv_b, wa, ba, wx, bx, lam):
    xf = causal_dwconv(x_br, conv_w, conv_b).astype(jnp.float32)
    r = jax.nn.sigmoid(block_diag(xf, wa, ba))
    i = jax.nn.sigmoid(block_diag(xf, wx, bx))
    log_a = -RG_C * r * jax.nn.softplus(-lam.astype(jnp.float32))
    a = jnp.exp(log_a)
    u = jnp.sqrt(-jnp.expm1(2.0 * log_a)) * (i * xf)

    def combine(left, right):
        a1, b1 = left
        a2, b2 = right
        return a1 * a2, a2 * b1 + b2

    _, h = lax.associative_scan(combine, (a, u), axis=1)
    return h * jax.nn.gelu(y_br.astype(jnp.float32))


def fox_attention(q, k, v, f_logit):
    Bsz, T, _ = q.shape

    def heads(t):
        return t.astype(jnp.float32).reshape(Bsz, T, C_HEADS, C_HEAD_DIM).transpose(0, 2, 1, 3)

    qh = heads(q) * (C_HEAD_DIM ** -0.5)
    kh, vh = heads(k), heads(v)
    c = jnp.cumsum(jax.nn.log_sigmoid(f_logit.astype(jnp.float32)), axis=1).transpose(0, 2, 1)
    outs = []
    for blk in range(T // Q_BLOCK):
        s0 = blk * Q_BLOCK
        end = s0 + Q_BLOCK
        logits = jnp.einsum('bhqd,bhkd->bhqk', qh[:, :, s0:end], kh[:, :, :end])
        logits = logits + c[:, :, s0:end, None] - c[:, :, None, :end]
        mask = (s0 + jnp.arange(Q_BLOCK))[:, None] >= jnp.arange(end)[None, :]
        p = jax.nn.softmax(jnp.where(mask, logits, -jnp.inf), axis=-1)
        outs.append(jnp.einsum('bhqk,bhkd->bhqd', p, vh[:, :, :end]))
    o = jnp.concatenate(outs, axis=2)
    return o.transpose(0, 2, 1, 3).reshape(Bsz, T, D_MODEL)


def conv_ffn(x, w_up, conv_w, conv_b, w_down):
    hid = causal_dwconv(x @ w_up, conv_w, conv_b)
    gate, val = jnp.split(hid, 2, axis=-1)
    return (jax.nn.gelu(gate) * val) @ w_down


def setup_inputs(seed: int = 0) -> dict:
    key = jax.random.key(seed)
    ks = jax.random.split(key, 20)
    f32 = jnp.float32
    nrm = jax.random.normal
    a0 = jax.random.uniform(ks[11], (N_EVEN, B_WIDTH), f32, 0.9 ** (1.0 / RG_C), 0.999 ** (1.0 / RG_C))
    return {
        'x': nrm(ks[0], (BATCH, SEQ, D_MODEL), f32),
        'norm_gains': 1.0 + 0.1 * nrm(ks[1], (DEPTH, 4, D_MODEL), f32),
        'even_w_in': nrm(ks[2], (N_EVEN, D_MODEL, EVEN_IN), f32) * D_MODEL ** -0.5,
        'hgrn_lb_logits': 0.1 * nrm(ks[3], (DEPTH + 1, A_KEY_WIDTH), f32),
        'hgrn_norm': 1.0 + 0.1 * nrm(ks[4], (N_EVEN, A_WIDTH), f32),
        'rg_conv_w': nrm(ks[5], (N_EVEN, B_CONV, B_WIDTH), f32) * B_CONV ** -0.5,
        'rg_conv_b': 0.01 * nrm(ks[6], (N_EVEN, B_WIDTH), f32),
        'rg_wa': nrm(ks[7], (N_EVEN, B_BLOCKS, B_BLOCK_DIM, B_BLOCK_DIM), f32) * B_BLOCK_DIM ** -0.5,
        'rg_ba': 0.01 * nrm(ks[8], (N_EVEN, B_WIDTH), f32),
        'rg_wx': nrm(ks[9], (N_EVEN, B_BLOCKS, B_BLOCK_DIM, B_BLOCK_DIM), f32) * B_BLOCK_DIM ** -0.5,
        'rg_bx': 0.01 * nrm(ks[10], (N_EVEN, B_WIDTH), f32),
        'rg_lambda': jnp.log(a0) - jnp.log1p(-a0),
        'even_w_out': nrm(ks[12], (N_EVEN, A_WIDTH + B_WIDTH, D_MODEL), f32) * (A_WIDTH + B_WIDTH) ** -0.5,
        'odd_w_in': nrm(ks[13], (N_ODD, D_MODEL, ODD_IN), f32) * D_MODEL ** -0.5,
        'fox_f_bias': jax.random.uniform(ks[14], (N_ODD, C_HEADS), f32, 1.0, 5.0),
        'odd_w_out': nrm(ks[15], (N_ODD, D_MODEL, D_MODEL), f32) * D_MODEL ** -0.5,
        'ffn_w_up': nrm(ks[16], (DEPTH, D_MODEL, 2 * D_FF), f32) * D_MODEL ** -0.5,
        'ffn_conv_w': nrm(ks[17], (DEPTH, FFN_CONV, 2 * D_FF), f32) * FFN_CONV ** -0.5,
        'ffn_conv_b': 0.01 * nrm(ks[18], (DEPTH, 2 * D_FF), f32),
        'ffn_w_down': nrm(ks[19], (DEPTH, D_FF, D_MODEL), f32) * D_FF ** -0.5,
    }


def reference(x, norm_gains, even_w_in, hgrn_lb_logits, hgrn_norm, rg_conv_w, rg_conv_b, rg_wa, rg_ba,
              rg_wx, rg_bx, rg_lambda, even_w_out, odd_w_in, fox_f_bias, odd_w_out,
              ffn_w_up, ffn_conv_w, ffn_conv_b, ffn_w_down):
    lb_all = jnp.cumsum(jax.nn.softmax(hgrn_lb_logits.astype(jnp.float32), axis=0), axis=0)
    for l in range(DEPTH):
        g = norm_gains[l]
        h = rms_norm(x, g[0])
        if l % 2 == 0:
            e = l // 2
            qa, fa, ia, ga, xb, yb = split_cols(h @ even_w_in[e], EVEN_SIZES)
            oa = hgrn2_mix(qa, fa, ia, ga, lb_all[l], hgrn_norm[e])
            ob = rglru_mix(xb, yb, rg_conv_w[e], rg_conv_b[e], rg_wa[e], rg_ba[e], rg_wx[e], rg_bx[e], rg_lambda[e])
            mix = jnp.concatenate([oa, ob], axis=-1).astype(h.dtype) @ even_w_out[e]
        else:
            o = l // 2
            qc, kc, vc, fc = split_cols(h @ odd_w_in[o], ODD_SIZES)
            mix = fox_attention(qc, kc, vc, fc + fox_f_bias[o]).astype(h.dtype) @ odd_w_out[o]
        x = x + rms_norm(mix, g[1])
        h = rms_norm(x, g[2])
        x = x + rms_norm(conv_ffn(h, ffn_w_up[l], ffn_conv_w[l], ffn_conv_b[l], ffn_w_down[l]), g[3])
    return x
```

```python
import functools

import jax
import jax.numpy as jnp
from jax import lax
from jax.experimental import pallas as pl
from jax.experimental.pallas import tpu as pltpu

F32 = jnp.float32
BF16 = jnp.bfloat16

D_MODEL = 1024
DEPTH = 2
A_HEADS = 4
A_DIM = 128
A_WIDTH = A_HEADS * A_DIM
HGRN_CHUNK = 64
B_WIDTH = D_MODEL - A_WIDTH
B_BLOCKS = 8
B_BLOCK_DIM = B_WIDTH // B_BLOCKS
B_CONV = 4
RG_C = 8.0
C_HEADS = 16
C_HEAD_DIM = D_MODEL // C_HEADS
D_FF = 2816
FFN_CONV = 3
EPS = 1e-6

LANES = 128
SUBLANES = 8
VMEM_LIMIT = 56 * 1024 * 1024

ROW_TILE = 512
SEQ_TILE = 256
FFN_TILE = 512
FF_CHUNK = 256
ATT_TILE = 256
NEG_BIG = -1e30

GELU_C0 = 0.7978845608028654
GELU_C1 = 0.044715


def _cparams(semantics):
    return pltpu.CompilerParams(dimension_semantics=semantics, vmem_limit_bytes=VMEM_LIMIT)


def _rms_rows(x, gain):
    ms = jnp.mean(x * x, axis=-1, keepdims=True)
    return x * lax.rsqrt(ms + EPS) * gain


def _gelu_tanh(x):
    inner = x * (GELU_C0 + (GELU_C0 * GELU_C1) * (x * x))
    return 0.5 * x * (1.0 + jnp.tanh(inner))


def _sigmoid(x):
    return 1.0 / (1.0 + jnp.exp(-x))


def _dot(a, b):
    return jnp.dot(a, b, preferred_element_type=F32)


def _dot_nt(a, b):
    return lax.dot_general(a, b, (((1,), (1,)), ((), ())), preferred_element_type=F32)


def _dot_tn(a, b):
    return lax.dot_general(a, b, (((0,), (0,)), ((), ())), preferred_element_type=F32)


def _split_bf16(x):
    hi = x.astype(BF16)
    lo = (x - hi.astype(F32)).astype(BF16)
    return hi, lo


def _even_inproj_kernel(x_ref, g_ref, w_ref, o_ref, *, col_chunk):
    h = _rms_rows(x_ref[...], g_ref[...]).astype(BF16)
    for c0 in range(0, o_ref.shape[-1], col_chunk):
        o_ref[:, c0:c0 + col_chunk] = _dot(h, w_ref[:, c0:c0 + col_chunk])


def _even_inproj(x2, gain, w_bf16):
    n, d = x2.shape
    n_out = w_bf16.shape[1]
    return pl.pallas_call(
        functools.partial(_even_inproj_kernel, col_chunk=512),
        grid=(n // ROW_TILE,),
        in_specs=[
            pl.BlockSpec((ROW_TILE, d), lambda i: (i, 0)),
            pl.BlockSpec((1, d), lambda i: (0, 0)),
            pl.BlockSpec((d, n_out), lambda i: (0, 0)),
        ],
        out_specs=pl.BlockSpec((ROW_TILE, n_out), lambda i: (i, 0)),
        out_shape=jax.ShapeDtypeStruct((n, n_out), F32),
        compiler_params=_cparams(("parallel",)),
        name="even_inproj",
    )(x2, gain, w_bf16)


def _odd_inproj_kernel(x_ref, g_ref, wqkv_ref, wf_ref, q_ref, k_ref, v_ref, f_ref):
    h = _rms_rows(x_ref[...], g_ref[...]).astype(BF16)
    d = q_ref.shape[-1]
    q_ref[...] = (_dot(h, wqkv_ref[:, 0:d]) * (C_HEAD_DIM ** -0.5)).astype(BF16)
    k_ref[...] = _dot(h, wqkv_ref[:, d:2 * d]).astype(BF16)
    v_ref[...] = _dot(h, wqkv_ref[:, 2 * d:3 * d]).astype(BF16)
    f_ref[...] = _dot(h, wf_ref[...])


def _odd_inproj(x2, gain, wqkv_bf16, wf_bf16):
    n, d = x2.shape
    row = lambda i: (i, 0)
    fixed = lambda i: (0, 0)
    return pl.pallas_call(
        _odd_inproj_kernel,
        grid=(n // ROW_TILE,),
        in_specs=[
            pl.BlockSpec((ROW_TILE, d), row),
            pl.BlockSpec((1, d), fixed),
            pl.BlockSpec((d, 3 * d), fixed),
            pl.BlockSpec((d, LANES), fixed),
        ],
        out_specs=[
            pl.BlockSpec((ROW_TILE, d), row),
            pl.BlockSpec((ROW_TILE, d), row),
            pl.BlockSpec((ROW_TILE, d), row),
            pl.BlockSpec((ROW_TILE, LANES), row),
        ],
        out_shape=[
            jax.ShapeDtypeStruct((n, d), BF16),
            jax.ShapeDtypeStruct((n, d), BF16),
            jax.ShapeDtypeStruct((n, d), BF16),
            jax.ShapeDtypeStruct((n, LANES), F32),
        ],
        compiler_params=_cparams(("parallel",)),
        name="odd_inproj",
    )(x2, gain, wqkv_bf16, wf_bf16)


def _outproj_kernel(*refs, n_lhs):
    a_refs = refs[:n_lhs]
    w_refs = refs[n_lhs:2 * n_lhs]
    x_ref, g_ref, o_ref = refs[2 * n_lhs:]
    mix = _dot(a_refs[0][...].astype(BF16), w_refs[0][...])
    for a_ref, w_ref in zip(a_refs[1:], w_refs[1:]):
        mix = mix + _dot(a_ref[...].astype(BF16), w_ref[...])
    o_ref[...] = x_ref[...] + _rms_rows(mix, g_ref[...])


def _outproj(lhs_list, w_list, x2, gain):
    n, d = x2.shape
    n_lhs = len(lhs_list)
    row = lambda i: (i, 0)
    fixed = lambda i: (0, 0)
    in_specs = [pl.BlockSpec((ROW_TILE, a.shape[1]), row) for a in lhs_list]
    in_specs += [pl.BlockSpec(w.shape, fixed) for w in w_list]
    in_specs += [pl.BlockSpec((ROW_TILE, d), row), pl.BlockSpec((1, d), fixed)]
    return pl.pallas_call(
        functools.partial(_outproj_kernel, n_lhs=n_lhs),
        grid=(n // ROW_TILE,),
        in_specs=in_specs,
        out_specs=pl.BlockSpec((ROW_TILE, d), row),
        out_shape=jax.ShapeDtypeStruct((n, d), F32),
        compiler_params=_cparams(("parallel",)),
        name="outproj",
    )(*lhs_list, *w_list, x2, gain)


def _hgrn_kernel(q_ref, f_ref, i_ref, g_ref, lbz_ref, gain_ref, o_ref, st_ref, *, layer):
    tt = q_ref.shape[1]

    @pl.when(pl.program_id(1) == 0)
    def _():
        st_ref[...] = jnp.zeros_like(st_ref)

    z = lbz_ref[...]
    ez = jnp.exp(z - jnp.max(z, axis=0, keepdims=True))
    lb = jnp.sum(ez[:layer + 1], axis=0, keepdims=True) / jnp.sum(ez, axis=0, keepdims=True)

    forget = lb + (1.0 - lb) * _sigmoid(f_ref[0])
    logf = jnp.log(forget)

    r = lax.broadcasted_iota(jnp.int32, (tt, tt), 0)
    c = lax.broadcasted_iota(jnp.int32, (tt, tt), 1)
    shift = HGRN_CHUNK.bit_length() - 1
    same_chunk = lax.shift_right_logical(r, shift) == lax.shift_right_logical(c, shift)
    tri = jnp.where((c <= r) & same_chunk, 1.0, 0.0).astype(BF16)
    hi, lo = _split_bf16(logf)
    bcum = _dot(tri, hi) + _dot(tri, lo)

    qs = q_ref[0]
    qs = qs * _sigmoid(qs)
    kk = 1.0 - forget
    vv = i_ref[0]
    gate = _sigmoid(g_ref[0])
    gain = gain_ref[...]

    cr = lax.broadcasted_iota(jnp.int32, (HGRN_CHUNK, HGRN_CHUNK), 0)
    cc = lax.broadcasted_iota(jnp.int32, (HGRN_CHUNK, HGRN_CHUNK), 1)
    causal = cc <= cr

    for ch in range(tt // HGRN_CHUNK):
        rows = slice(ch * HGRN_CHUNK, (ch + 1) * HGRN_CHUNK)
        bc = bcum[rows]
        b_last = bc[HGRN_CHUNK - 1:HGRN_CHUNK]
        q_dec = (qs[rows] * jnp.exp(bc)).astype(BF16)
        k_intra = (kk[rows] * jnp.exp(-bc)).astype(BF16)
        k_upd = (kk[rows] * jnp.exp(b_last - bc)).astype(BF16)
        dec = jnp.exp(b_last)
        v_bf = vv[rows].astype(BF16)
        outs = []
        for hd in range(A_HEADS):
            cols = slice(hd * A_DIM, (hd + 1) * A_DIM)
            st = st_ref[hd]
            scores = jnp.where(causal, _dot_nt(q_dec[:, cols], k_intra[:, cols]), 0.0)
            o = _dot(scores.astype(BF16), v_bf[:, cols]) + _dot_nt(q_dec[:, cols], st.astype(BF16))
            st_ref[hd] = dec[:, cols] * st + _dot_tn(v_bf[:, cols], k_upd[:, cols])
            o = o * lax.rsqrt(jnp.mean(o * o, axis=-1, keepdims=True) + EPS)
            outs.append(o)
        o_all = jnp.concatenate(outs, axis=-1)
        o_ref[0, rows, :] = o_all * gain * gate[rows]


def _hgrn(proj3, lb_logits, norm_gain, layer):
    b, t, _ = proj3.shape
    spec = lambda j: pl.BlockSpec((1, SEQ_TILE, A_WIDTH), lambda bi, ti, j=j: (bi, ti, j))
    fixed = lambda bi, ti: (0, 0)
    return pl.pallas_call(
        functools.partial(_hgrn_kernel, layer=layer),
        grid=(b, t // SEQ_TILE),
        in_specs=[spec(0), spec(1), spec(2), spec(3),
                  pl.BlockSpec(lb_logits.shape, fixed),
                  pl.BlockSpec((1, A_WIDTH), fixed)],
        out_specs=pl.BlockSpec((1, SEQ_TILE, A_WIDTH), lambda bi, ti: (bi, ti, 0)),
        out_shape=jax.ShapeDtypeStruct((b, t, A_WIDTH), F32),
        scratch_shapes=[pltpu.VMEM((A_HEADS, A_DIM, A_DIM), F32)],
        compiler_params=_cparams(("parallel", "arbitrary")),
        name="hgrn2",
    )(proj3, proj3, proj3, proj3, lb_logits, norm_gain)


def _rglru_kernel(x_ref, y_ref, cw_ref, cb_ref, wg_ref, bg_ref, lam_ref, o_ref,
                  xext_ref, hprev_ref):
    tt = x_ref.shape[1]
    halo = SUBLANES

    @pl.when(pl.program_id(1) == 0)
    def _():
        xext_ref[0:halo, :] = jnp.zeros((halo, B_WIDTH), F32)
        hprev_ref[...] = jnp.zeros_like(hprev_ref)

    x = x_ref[0]
    xext_ref[halo:halo + tt, :] = x
    cw = cw_ref[...]
    xf = cb_ref[...] + cw[B_CONV - 1:B_CONV] * x
    for j in range(1, B_CONV):
        xf = xf + cw[B_CONV - 1 - j:B_CONV - j] * xext_ref[halo - j:halo - j + tt, :]
    xext_ref[0:halo, :] = x[tt - halo:tt]

    gates = _dot(xf.astype(BF16), wg_ref[...]) + bg_ref[...]
    r = _sigmoid(gates[:, :B_WIDTH])
    i = _sigmoid(gates[:, B_WIDTH:])
    lam = lam_ref[...]
    softplus_neg = jnp.maximum(-lam, 0.0) + jnp.log(1.0 + jnp.exp(-jnp.abs(lam)))
    log_a = (-RG_C) * r * softplus_neg
    a = jnp.exp(log_a)
    u = jnp.sqrt(1.0 - jnp.exp(2.0 * log_a)) * (i * xf)

    groups = tt // SUBLANES
    a3 = a.reshape(groups, SUBLANES, B_WIDTH)
    u3 = u.reshape(groups, SUBLANES, B_WIDTH)
    row = lax.broadcasted_iota(jnp.int32, (groups, SUBLANES, B_WIDTH), 1)
    d = 1
    while d < SUBLANES:
        keep = row >= d
        a_sh = jnp.where(keep, pltpu.roll(a3, d, 1), 1.0)
        u_sh = jnp.where(keep, pltpu.roll(u3, d, 1), 0.0)
        u3 = u3 + a3 * u_sh
        a3 = a3 * a_sh
        d *= 2
    carry = hprev_ref[...]
    hs = []
    for gi in range(groups):
        hg = a3[gi] * carry + u3[gi]
        hs.append(hg)
        carry = hg[SUBLANES - 1:SUBLANES]
    hprev_ref[...] = carry
    h = jnp.concatenate(hs, axis=0)
    o_ref[0] = h * _gelu_tanh(y_ref[0])


def _rglru(proj3, conv_w, conv_b, w_gates, b_gates, lam):
    b, t, _ = proj3.shape
    col0 = (4 * A_WIDTH) // B_WIDTH
    fixed = lambda bi, ti: (0, 0)
    return pl.pallas_call(
        _rglru_kernel,
        grid=(b, t // SEQ_TILE),
        in_specs=[
            pl.BlockSpec((1, SEQ_TILE, B_WIDTH), lambda bi, ti: (bi, ti, col0)),
            pl.BlockSpec((1, SEQ_TILE, B_WIDTH), lambda bi, ti: (bi, ti, col0 + 1)),
            pl.BlockSpec(conv_w.shape, fixed),
            pl.BlockSpec(conv_b.shape, fixed),
            pl.BlockSpec(w_gates.shape, fixed),
            pl.BlockSpec(b_gates.shape, fixed),
            pl.BlockSpec(lam.shape, fixed),
        ],
        out_specs=pl.BlockSpec((1, SEQ_TILE, B_WIDTH), lambda bi, ti: (bi, ti, 0)),
        out_shape=jax.ShapeDtypeStruct((b, t, B_WIDTH), F32),
        scratch_shapes=[pltpu.VMEM((SEQ_TILE + SUBLANES, B_WIDTH), F32),
                        pltpu.VMEM((1, B_WIDTH), F32)],
        compiler_params=_cparams(("parallel", "arbitrary")),
        name="rglru",
    )(proj3, proj3, conv_w, conv_b, w_gates, b_gates, lam)


def _block_diag_dense(w):
    nb, bd, _ = w.shape
    eye = jnp.eye(nb, dtype=w.dtype)
    return jnp.einsum('nij,nm->nimj', w, eye).reshape(nb * bd, nb * bd)


def _fox_cumsum_kernel(f_ref, b_ref, o_ref, *, blk):
    t, w = f_ref.shape
    z = f_ref[...] + b_ref[...]
    logsig = jnp.minimum(z, 0.0) - jnp.log(1.0 + jnp.exp(-jnp.abs(z)))
    r = lax.broadcasted_iota(jnp.int32, (blk, blk), 0)
    c = lax.broadcasted_iota(jnp.int32, (blk, blk), 1)
    tri = jnp.where(c <= r, 1.0, 0.0).astype(BF16)
    carry = jnp.zeros((1, w), F32)
    for s in range(0, t, blk):
        x = logsig[s:s + blk]
        hi = x.astype(BF16)
        r1 = x - hi.astype(F32)
        mid = r1.astype(BF16)
        lo = (r1 - mid.astype(F32)).astype(BF16)
        cs = _dot(tri, hi) + _dot(tri, mid) + _dot(tri, lo) + carry
        o_ref[s:s + blk, :] = cs
        carry = cs[blk - 1:blk]


def _fox_cumsum(f_tb, bias_row):
    t, w = f_tb.shape
    return pl.pallas_call(
        functools.partial(_fox_cumsum_kernel, blk=256),
        out_shape=jax.ShapeDtypeStruct((t, w), F32),
        compiler_params=pltpu.CompilerParams(vmem_limit_bytes=VMEM_LIMIT),
        name="fox_cumsum",
    )(f_tb, bias_row)


def _fox_attn_kernel(q_ref, k_ref, v_ref, c_ref, o_ref, m_ref, l_ref, acc_ref):
    tq = q_ref.shape[1]
    tk = tq
    qi = pl.program_id(2)
    lane = lax.broadcasted_iota(jnp.int32, (tq, LANES), 1)
    q2 = q_ref[0]
    rq = lax.broadcasted_iota(jnp.int32, (tq, tk), 0)
    ck = lax.broadcasted_iota(jnp.int32, (tq, tk), 1)
    diag_ok = ck <= rq

    for hd in range(2):
        head_lanes = (lane >= hd * C_HEAD_DIM) & (lane < (hd + 1) * C_HEAD_DIM)
        qh = jnp.where(head_lanes, q2, jnp.zeros_like(q2))
        m_ref[hd] = jnp.full((tq, LANES), NEG_BIG, F32)
        l_ref[hd] = jnp.zeros((tq, LANES), F32)
        acc_ref[hd] = jnp.zeros((tq, LANES), F32)

        def step(j, masked, hd=hd, qh=qh):
            start = pl.multiple_of(j * tk, tk)
            kb = k_ref[0, pl.ds(start, tk), :]
            vb = v_ref[0, pl.ds(start, tk), :]
            s = _dot_nt(qh, kb) - c_ref[0, 0, hd:hd + 1, pl.ds(start, tk)]
            if masked:
                s = jnp.where(diag_ok, s, NEG_BIG)
            m_prev = m_ref[hd]
            m_new = jnp.maximum(m_prev, jnp.max(s, axis=-1, keepdims=True))
            alpha = jnp.exp(m_prev - m_new)
            p = jnp.exp(s - m_new[:, 0:1])
            l_ref[hd] = alpha * l_ref[hd] + jnp.sum(p, axis=-1, keepdims=True)
            acc_ref[hd] = alpha * acc_ref[hd] + _dot(p.astype(BF16), vb)
            m_ref[hd] = m_new

        def body(j, carry, step=step):
            step(j, False)
            return carry

        lax.fori_loop(0, qi, body, 0)
        step(qi, True)

    o0 = acc_ref[0] / l_ref[0]
    o1 = acc_ref[1] / l_ref[1]
    o_ref[0] = jnp.where(lane < C_HEAD_DIM, o0, o1).astype(o_ref.dtype)


def _fox_attn(q3, k3, v3, c4):
    b, t, d = q3.shape
    pairs = d // LANES
    return pl.pallas_call(
        _fox_attn_kernel,
        grid=(b, pairs, t // ATT_TILE),
        in_specs=[
            pl.BlockSpec((1, ATT_TILE, LANES), lambda bi, hp, qi: (bi, qi, hp)),
            pl.BlockSpec((1, t, LANES), lambda bi, hp, qi: (bi, 0, hp)),
            pl.BlockSpec((1, t, LANES), lambda bi, hp, qi: (bi, 0, hp)),
            pl.BlockSpec((1, 1, 2, t), lambda bi, hp, qi: (bi, hp, 0, 0)),
        ],
        out_specs=pl.BlockSpec((1, ATT_TILE, LANES), lambda bi, hp, qi: (bi, qi, hp)),
        out_shape=jax.ShapeDtypeStruct((b, t, d), BF16),
        scratch_shapes=[pltpu.VMEM((2, ATT_TILE, LANES), F32),
                        pltpu.VMEM((2, ATT_TILE, LANES), F32),
                        pltpu.VMEM((2, ATT_TILE, LANES), F32)],
        compiler_params=_cparams(("parallel", "parallel", "arbitrary")),
        name="fox_attn",
    )(q3, k3, v3, c4)


def _ffn_kernel(x_ref, g_in_ref, g_out_ref, wup_ref, cw_ref, wdn_ref, o_ref,
                h_ref, ext_ref, prev_ref, acc_ref):
    tm = x_ref.shape[1]
    halo = SUBLANES
    n_chunks = wup_ref.shape[0]
    fc = wdn_ref.shape[1]

    @pl.when(pl.program_id(1) == 0)
    def _():
        prev_ref[...] = jnp.zeros_like(prev_ref)

    x = x_ref[0]
    h_ref[...] = _rms_rows(x, g_in_ref[...]).astype(BF16)
    acc_ref[...] = jnp.zeros_like(acc_ref)

    def chunk(j, carry):
        up = _dot(h_ref[...], wup_ref[j])
        ext_ref[0:halo, :] = prev_ref[j]
        ext_ref[halo:halo + tm, :] = up
        prev_ref[j] = up[tm - halo:tm]
        cw = cw_ref[j]
        conv = cw[FFN_CONV:FFN_CONV + 1] + cw[FFN_CONV - 1:FFN_CONV] * up
        for s in range(1, FFN_CONV):
            conv = conv + cw[FFN_CONV - 1 - s:FFN_CONV - s] * ext_ref[halo - s:halo - s + tm, :]
        act = (_gelu_tanh(conv[:, :fc]) * conv[:, fc:]).astype(BF16)
        acc_ref[...] += _dot(act, wdn_ref[j])
        return carry

    lax.fori_loop(0, n_chunks, chunk, 0)
    o_ref[0] = x + _rms_rows(acc_ref[...], g_out_ref[...])


def _ffn(x3, g_in, g_out, wup_c, cw_c, wdn_c):
    b, t, d = x3.shape
    n_chunks, _, fc2 = wup_c.shape
    fixed2 = lambda bi, ti: (0, 0)
    fixed3 = lambda bi, ti: (0, 0, 0)
    return pl.pallas_call(
        _ffn_kernel,
        grid=(b, t // FFN_TILE),
        in_specs=[
            pl.BlockSpec((1, FFN_TILE, d), lambda bi, ti: (bi, ti, 0)),
            pl.BlockSpec((1, d), fixed2),
            pl.BlockSpec((1, d), fixed2),
            pl.BlockSpec(wup_c.shape, fixed3, pipeline_mode=pl.Buffered(1)),
            pl.BlockSpec(cw_c.shape, fixed3, pipeline_mode=pl.Buffered(1)),
            pl.BlockSpec(wdn_c.shape, fixed3, pipeline_mode=pl.Buffered(1)),
        ],
        out_specs=pl.BlockSpec((1, FFN_TILE, d), lambda bi, ti: (bi, ti, 0)),
        out_shape=jax.ShapeDtypeStruct((b, t, d), F32),
        scratch_shapes=[
            pltpu.VMEM((FFN_TILE, d), BF16),
            pltpu.VMEM((FFN_TILE + SUBLANES, fc2), F32),
            pltpu.VMEM((n_chunks, SUBLANES, fc2), F32),
            pltpu.VMEM((FFN_TILE, d), F32),
        ],
        compiler_params=_cparams(("parallel", "arbitrary")),
        name="conv_ffn",
    )(x3, g_in, g_out, wup_c, cw_c, wdn_c)


def _ffn_weights(w_up, conv_w, conv_b, w_down):
    d = w_up.shape[0]
    n_chunks = D_FF // FF_CHUNK
    wg = w_up[:, :D_FF].reshape(d, n_chunks, FF_CHUNK)
    wv = w_up[:, D_FF:].reshape(d, n_chunks, FF_CHUNK)
    wup_c = jnp.concatenate([wg, wv], axis=-1).transpose(1, 0, 2).astype(BF16)
    taps = jnp.concatenate([conv_w, conv_b[None, :]], axis=0)
    taps = jnp.pad(taps, ((0, SUBLANES - taps.shape[0]), (0, 0)))
    tg = taps[:, :D_FF].reshape(SUBLANES, n_chunks, FF_CHUNK)
    tv = taps[:, D_FF:].reshape(SUBLANES, n_chunks, FF_CHUNK)
    cw_c = jnp.concatenate([tg, tv], axis=-1).transpose(1, 0, 2)
    wdn_c = w_down.reshape(n_chunks, FF_CHUNK, d).astype(BF16)
    return wup_c, cw_c, wdn_c


def kernel(x, norm_gains, even_w_in, hgrn_lb_logits, hgrn_norm, rg_conv_w, rg_conv_b, rg_wa, rg_ba,
           rg_wx, rg_bx, rg_lambda, even_w_out, odd_w_in, fox_f_bias, odd_w_out,
           ffn_w_up, ffn_conv_w, ffn_conv_b, ffn_w_down):
    b, t, d = x.shape
    n = b * t
    x2 = x.reshape(n, d)
    for l in range(DEPTH):
        g = norm_gains[l]
        if l % 2 == 0:
            e = l // 2
            proj = _even_inproj(x2, g[0:1], even_w_in[e].astype(BF16))
            proj3 = proj.reshape(b, t, proj.shape[1])
            oa = _hgrn(proj3, hgrn_lb_logits, hgrn_norm[e][None, :], l)
            w_gates = jnp.concatenate(
                [_block_diag_dense(rg_wa[e]), _block_diag_dense(rg_wx[e])], axis=1).astype(BF16)
            b_gates = jnp.concatenate([rg_ba[e], rg_bx[e]])[None, :]
            ob = _rglru(proj3, rg_conv_w[e], rg_conv_b[e][None, :], w_gates, b_gates,
                        rg_lambda[e][None, :])
            w_out = even_w_out[e].astype(BF16)
            x2 = _outproj([oa.reshape(n, A_WIDTH), ob.reshape(n, B_WIDTH)],
                          [w_out[:A_WIDTH], w_out[A_WIDTH:]], x2, g[1:2])
        else:
            o = l // 2
            w_in = odd_w_in[o]
            wqkv = w_in[:, :3 * d].astype(BF16)
            wf = jnp.pad(w_in[:, 3 * d:], ((0, 0), (0, LANES - C_HEADS))).astype(BF16)
            q, k, v, f = _odd_inproj(x2, g[0:1], wqkv, wf)
            f_tb = f.reshape(b, t, LANES)[:, :, :C_HEADS].transpose(1, 0, 2).reshape(t, b * C_HEADS)
            bias_row = jnp.tile(fox_f_bias[o], b)[None, :]
            c_tb = _fox_cumsum(f_tb, bias_row)
            c4 = c_tb.reshape(t, b, C_HEADS // 2, 2).transpose(1, 2, 3, 0)
            att = _fox_attn(q.reshape(b, t, d), k.reshape(b, t, d), v.reshape(b, t, d), c4)
            x2 = _outproj([att.reshape(n, d)], [odd_w_out[o].astype(BF16)], x2, g[1:2])
        wup_c, cw_c, wdn_c = _ffn_weights(ffn_w_up[l], ffn_conv_w[l], ffn_conv_b[l], ffn_w_down[l])
        x2 = _ffn(x2.reshape(b, t, d), g[2:3], g[3:4], wup_c, cw_c, wdn_c).reshape(n, d)
    return x2.reshape(b, t, d)
```

```python
import functools

import jax
import jax.numpy as jnp
from jax import lax
from jax.experimental import pallas as pl
from jax.experimental.pallas import tpu as pltpu

F32 = jnp.float32
BF16 = jnp.bfloat16

D_MODEL = 1024
DEPTH = 2
A_HEADS = 4
A_DIM = 128
A_WIDTH = A_HEADS * A_DIM
HGRN_CHUNK = 64
B_WIDTH = D_MODEL - A_WIDTH
B_BLOCKS = 8
B_BLOCK_DIM = B_WIDTH // B_BLOCKS
B_CONV = 4
RG_C = 8.0
C_HEADS = 16
C_HEAD_DIM = D_MODEL // C_HEADS
D_FF = 2816
FFN_CONV = 3
EPS = 1e-6

LANES = 128
SUBLANES = 8
VMEM_LIMIT = 56 * 1024 * 1024

ROW_TILE = 512
SEQ_TILE = 256
FFN_TILE = 512
FF_CHUNK = 256
ATT_TILE = 256
ATT_PAIRS = 4
NEG_BIG = -1e30

GELU_C0 = 0.7978845608028654
GELU_C1 = 0.044715


def _cparams(semantics):
    return pltpu.CompilerParams(dimension_semantics=semantics, vmem_limit_bytes=VMEM_LIMIT)


def _rms_rows(x, gain):
    ms = jnp.mean(x * x, axis=-1, keepdims=True)
    return x * lax.rsqrt(ms + EPS) * gain


def _gelu_tanh(x):
    inner = x * (GELU_C0 + (GELU_C0 * GELU_C1) * (x * x))
    return 0.5 * x * (1.0 + jnp.tanh(inner))


def _sigmoid(x):
    return 1.0 / (1.0 + jnp.exp(-x))


def _dot(a, b):
    return jnp.dot(a, b, preferred_element_type=F32)


def _dot_nt(a, b):
    return lax.dot_general(a, b, (((1,), (1,)), ((), ())), preferred_element_type=F32)


def _dot_tn(a, b):
    return lax.dot_general(a, b, (((0,), (0,)), ((), ())), preferred_element_type=F32)


def _split_bf16(x):
    hi = x.astype(BF16)
    lo = (x - hi.astype(F32)).astype(BF16)
    return hi, lo


def _even_inproj_kernel(x_ref, g_ref, w_ref, o_ref, *, col_chunk):
    h = _rms_rows(x_ref[...], g_ref[...]).astype(BF16)
    for c0 in range(0, o_ref.shape[-1], col_chunk):
        o_ref[:, c0:c0 + col_chunk] = _dot(h, w_ref[:, c0:c0 + col_chunk])


def _even_inproj(x2, gain, w_bf16):
    n, d = x2.shape
    n_out = w_bf16.shape[1]
    return pl.pallas_call(
        functools.partial(_even_inproj_kernel, col_chunk=512),
        grid=(n // ROW_TILE,),
        in_specs=[
            pl.BlockSpec((ROW_TILE, d), lambda i: (i, 0)),
            pl.BlockSpec((1, d), lambda i: (0, 0)),
            pl.BlockSpec((d, n_out), lambda i: (0, 0)),
        ],
        out_specs=pl.BlockSpec((ROW_TILE, n_out), lambda i: (i, 0)),
        out_shape=jax.ShapeDtypeStruct((n, n_out), F32),
        compiler_params=_cparams(("parallel",)),
        name="even_inproj",
    )(x2, gain, w_bf16)


def _odd_inproj_kernel(x_ref, g_ref, wqk_ref, wvt_ref, wf_ref, q_ref, k_ref, vt_ref, f_ref):
    h = _rms_rows(x_ref[...], g_ref[...]).astype(BF16)
    d = q_ref.shape[-1]
    q_ref[...] = (_dot(h, wqk_ref[:, 0:d]) * (C_HEAD_DIM ** -0.5)).astype(BF16)
    k_ref[...] = _dot(h, wqk_ref[:, d:2 * d]).astype(BF16)
    vt_ref[...] = _dot_nt(wvt_ref[...], h).astype(BF16)
    f_ref[...] = _dot(h, wf_ref[...])


def _odd_inproj(x2, gain, wqk_bf16, wvt_bf16, wf_bf16):
    n, d = x2.shape
    row = lambda i: (i, 0)
    fixed = lambda i: (0, 0)
    return pl.pallas_call(
        _odd_inproj_kernel,
        grid=(n // ROW_TILE,),
        in_specs=[
            pl.BlockSpec((ROW_TILE, d), row),
            pl.BlockSpec((1, d), fixed),
            pl.BlockSpec((d, 2 * d), fixed),
            pl.BlockSpec((d, d), fixed),
            pl.BlockSpec((d, LANES), fixed),
        ],
        out_specs=[
            pl.BlockSpec((ROW_TILE, d), row),
            pl.BlockSpec((ROW_TILE, d), row),
            pl.BlockSpec((d, ROW_TILE), lambda i: (0, i)),
            pl.BlockSpec((ROW_TILE, LANES), row),
        ],
        out_shape=[
            jax.ShapeDtypeStruct((n, d), BF16),
            jax.ShapeDtypeStruct((n, d), BF16),
            jax.ShapeDtypeStruct((d, n), BF16),
            jax.ShapeDtypeStruct((n, LANES), F32),
        ],
        compiler_params=_cparams(("parallel",)),
        name="odd_inproj",
    )(x2, gain, wqk_bf16, wvt_bf16, wf_bf16)


def _outproj_kernel(*refs, n_lhs):
    a_refs = refs[:n_lhs]
    w_refs = refs[n_lhs:2 * n_lhs]
    x_ref, g_ref, o_ref = refs[2 * n_lhs:]
    mix = _dot(a_refs[0][...].astype(BF16), w_refs[0][...])
    for a_ref, w_ref in zip(a_refs[1:], w_refs[1:]):
        mix = mix + _dot(a_ref[...].astype(BF16), w_ref[...])
    o_ref[...] = x_ref[...] + _rms_rows(mix, g_ref[...])


def _outproj(lhs_list, w_list, x2, gain):
    n, d = x2.shape
    n_lhs = len(lhs_list)
    row = lambda i: (i, 0)
    fixed = lambda i: (0, 0)
    in_specs = [pl.BlockSpec((ROW_TILE, a.shape[1]), row) for a in lhs_list]
    in_specs += [pl.BlockSpec(w.shape, fixed) for w in w_list]
    in_specs += [pl.BlockSpec((ROW_TILE, d), row), pl.BlockSpec((1, d), fixed)]
    return pl.pallas_call(
        functools.partial(_outproj_kernel, n_lhs=n_lhs),
        grid=(n // ROW_TILE,),
        in_specs=in_specs,
        out_specs=pl.BlockSpec((ROW_TILE, d), row),
        out_shape=jax.ShapeDtypeStruct((n, d), F32),
        compiler_params=_cparams(("parallel",)),
        name="outproj",
    )(*lhs_list, *w_list, x2, gain)


def _hgrn_kernel(q_ref, f_ref, i_ref, g_ref, lbz_ref, gain_ref, o_ref, st_ref, *, layer):
    tt = q_ref.shape[1]

    @pl.when(pl.program_id(1) == 0)
    def _():
        st_ref[...] = jnp.zeros_like(st_ref)

    z = lbz_ref[...]
    ez = jnp.exp(z - jnp.max(z, axis=0, keepdims=True))
    lb = jnp.sum(ez[:layer + 1], axis=0, keepdims=True) / jnp.sum(ez, axis=0, keepdims=True)

    forget = lb + (1.0 - lb) * _sigmoid(f_ref[0])
    logf = jnp.log(forget)

    r = lax.broadcasted_iota(jnp.int32, (tt, tt), 0)
    c = lax.broadcasted_iota(jnp.int32, (tt, tt), 1)
    shift = HGRN_CHUNK.bit_length() - 1
    same_chunk = lax.shift_right_logical(r, shift) == lax.shift_right_logical(c, shift)
    tri = jnp.where((c <= r) & same_chunk, 1.0, 0.0).astype(BF16)
    hi, lo = _split_bf16(logf)
    bcum = _dot(tri, hi) + _dot(tri, lo)

    qs = q_ref[0]
    qs = qs * _sigmoid(qs)
    kk = 1.0 - forget
    vv = i_ref[0]
    gate = _sigmoid(g_ref[0])
    gain = gain_ref[...]

    cr = lax.broadcasted_iota(jnp.int32, (HGRN_CHUNK, HGRN_CHUNK), 0)
    cc = lax.broadcasted_iota(jnp.int32, (HGRN_CHUNK, HGRN_CHUNK), 1)
    causal = cc <= cr

    for ch in range(tt // HGRN_CHUNK):
        rows = slice(ch * HGRN_CHUNK, (ch + 1) * HGRN_CHUNK)
        bc = bcum[rows]
        b_last = bc[HGRN_CHUNK - 1:HGRN_CHUNK]
        q_dec = (qs[rows] * jnp.exp(bc)).astype(BF16)
        k_intra = (kk[rows] * jnp.exp(-bc)).astype(BF16)
        k_upd = (kk[rows] * jnp.exp(b_last - bc)).astype(BF16)
        dec = jnp.exp(b_last)
        v_bf = vv[rows].astype(BF16)
        outs = []
        for hd in range(A_HEADS):
            cols = slice(hd * A_DIM, (hd + 1) * A_DIM)
            st = st_ref[hd]
            scores = jnp.where(causal, _dot_nt(q_dec[:, cols], k_intra[:, cols]), 0.0)
            o = _dot(scores.astype(BF16), v_bf[:, cols]) + _dot_nt(q_dec[:, cols], st.astype(BF16))
            st_ref[hd] = dec[:, cols] * st + _dot_tn(v_bf[:, cols], k_upd[:, cols])
            o = o * lax.rsqrt(jnp.mean(o * o, axis=-1, keepdims=True) + EPS)
            outs.append(o)
        o_all = jnp.concatenate(outs, axis=-1)
        o_ref[0, rows, :] = o_all * gain * gate[rows]


def _hgrn(proj3, lb_logits, norm_gain, layer):
    b, t, _ = proj3.shape
    spec = lambda j: pl.BlockSpec((1, SEQ_TILE, A_WIDTH), lambda bi, ti, j=j: (bi, ti, j))
    fixed = lambda bi, ti: (0, 0)
    return pl.pallas_call(
        functools.partial(_hgrn_kernel, layer=layer),
        grid=(b, t // SEQ_TILE),
        in_specs=[spec(0), spec(1), spec(2), spec(3),
                  pl.BlockSpec(lb_logits.shape, fixed),
                  pl.BlockSpec((1, A_WIDTH), fixed)],
        out_specs=pl.BlockSpec((1, SEQ_TILE, A_WIDTH), lambda bi, ti: (bi, ti, 0)),
        out_shape=jax.ShapeDtypeStruct((b, t, A_WIDTH), F32),
        scratch_shapes=[pltpu.VMEM((A_HEADS, A_DIM, A_DIM), F32)],
        compiler_params=_cparams(("parallel", "arbitrary")),
        name="hgrn2",
    )(proj3, proj3, proj3, proj3, lb_logits, norm_gain)


def _rglru_kernel(x_ref, y_ref, cw_ref, cb_ref, wg_ref, bg_ref, lam_ref, o_ref,
                  xext_ref, hprev_ref):
    tt = x_ref.shape[1]
    halo = SUBLANES

    @pl.when(pl.program_id(1) == 0)
    def _():
        xext_ref[0:halo, :] = jnp.zeros((halo, B_WIDTH), F32)
        hprev_ref[...] = jnp.zeros_like(hprev_ref)

    x = x_ref[0]
    xext_ref[halo:halo + tt, :] = x
    cw = cw_ref[...]
    xf = cb_ref[...] + cw[B_CONV - 1:B_CONV] * x
    for j in range(1, B_CONV):
        xf = xf + cw[B_CONV - 1 - j:B_CONV - j] * xext_ref[halo - j:halo - j + tt, :]
    xext_ref[0:halo, :] = x[tt - halo:tt]

    gates = _dot(xf.astype(BF16), wg_ref[...]) + bg_ref[...]
    r = _sigmoid(gates[:, :B_WIDTH])
    i = _sigmoid(gates[:, B_WIDTH:])
    lam = lam_ref[...]
    softplus_neg = jnp.maximum(-lam, 0.0) + jnp.log(1.0 + jnp.exp(-jnp.abs(lam)))
    log_a = (-RG_C) * r * softplus_neg
    a = jnp.exp(log_a)
    u = jnp.sqrt(1.0 - jnp.exp(2.0 * log_a)) * (i * xf)

    groups = tt // SUBLANES
    a3 = a.reshape(groups, SUBLANES, B_WIDTH)
    u3 = u.reshape(groups, SUBLANES, B_WIDTH)
    row = lax.broadcasted_iota(jnp.int32, (groups, SUBLANES, B_WIDTH), 1)
    d = 1
    while d < SUBLANES:
        keep = row >= d
        a_sh = jnp.where(keep, pltpu.roll(a3, d, 1), 1.0)
        u_sh = jnp.where(keep, pltpu.roll(u3, d, 1), 0.0)
        u3 = u3 + a3 * u_sh
        a3 = a3 * a_sh
        d *= 2
    carry = hprev_ref[...]
    hs = []
    for gi in range(groups):
        hg = a3[gi] * carry + u3[gi]
        hs.append(hg)
        carry = hg[SUBLANES - 1:SUBLANES]
    hprev_ref[...] = carry
    h = jnp.concatenate(hs, axis=0)
    o_ref[0] = h * _gelu_tanh(y_ref[0])


def _rglru(proj3, conv_w, conv_b, w_gates, b_gates, lam):
    b, t, _ = proj3.shape
    col0 = (4 * A_WIDTH) // B_WIDTH
    fixed = lambda bi, ti: (0, 0)
    return pl.pallas_call(
        _rglru_kernel,
        grid=(b, t // SEQ_TILE),
        in_specs=[
            pl.BlockSpec((1, SEQ_TILE, B_WIDTH), lambda bi, ti: (bi, ti, col0)),
            pl.BlockSpec((1, SEQ_TILE, B_WIDTH), lambda bi, ti: (bi, ti, col0 + 1)),
            pl.BlockSpec(conv_w.shape, fixed),
            pl.BlockSpec(conv_b.shape, fixed),
            pl.BlockSpec(w_gates.shape, fixed),
            pl.BlockSpec(b_gates.shape, fixed),
            pl.BlockSpec(lam.shape, fixed),
        ],
        out_specs=pl.BlockSpec((1, SEQ_TILE, B_WIDTH), lambda bi, ti: (bi, ti, 0)),
        out_shape=jax.ShapeDtypeStruct((b, t, B_WIDTH), F32),
        scratch_shapes=[pltpu.VMEM((SEQ_TILE + SUBLANES, B_WIDTH), F32),
                        pltpu.VMEM((1, B_WIDTH), F32)],
        compiler_params=_cparams(("parallel", "arbitrary")),
        name="rglru",
    )(proj3, proj3, conv_w, conv_b, w_gates, b_gates, lam)


def _block_diag_dense(w):
    nb, bd, _ = w.shape
    eye = jnp.eye(nb, dtype=w.dtype)
    return jnp.einsum('nij,nm->nimj', w, eye).reshape(nb * bd, nb * bd)


def _fox_cumsum_kernel(f_ref, b_ref, o_ref, *, blk):
    t, w = f_ref.shape
    z = f_ref[...] + b_ref[...]
    logsig = jnp.minimum(z, 0.0) - jnp.log(1.0 + jnp.exp(-jnp.abs(z)))
    r = lax.broadcasted_iota(jnp.int32, (blk, blk), 0)
    c = lax.broadcasted_iota(jnp.int32, (blk, blk), 1)
    tri = jnp.where(c <= r, 1.0, 0.0).astype(BF16)
    carry = jnp.zeros((1, w), F32)
    for s in range(0, t, blk):
        x = logsig[s:s + blk]
        hi = x.astype(BF16)
        r1 = x - hi.astype(F32)
        mid = r1.astype(BF16)
        lo = (r1 - mid.astype(F32)).astype(BF16)
        cs = _dot(tri, hi) + _dot(tri, mid) + _dot(tri, lo) + carry
        o_ref[s:s + blk, :] = cs
        carry = cs[blk - 1:blk]


def _fox_cumsum(f_tb, bias_row):
    t, w = f_tb.shape
    return pl.pallas_call(
        functools.partial(_fox_cumsum_kernel, blk=256),
        out_shape=jax.ShapeDtypeStruct((t, w), F32),
        compiler_params=pltpu.CompilerParams(vmem_limit_bytes=VMEM_LIMIT),
        name="fox_cumsum",
    )(f_tb, bias_row)


def _fox_attn_kernel(q_ref, k_ref, vt_ref, c_ref, o_ref, qs_ref, m_ref, l_ref, acc_ref):
    tq = q_ref.shape[1]
    tk = tq
    n_pairs = q_ref.shape[2] // LANES
    qi = pl.program_id(2)
    dim = lax.broadcasted_iota(jnp.int32, (LANES, tq), 0)
    low = dim < C_HEAD_DIM
    key_row = lax.broadcasted_iota(jnp.int32, (tk, tq), 0)
    qry_col = lax.broadcasted_iota(jnp.int32, (tk, tq), 1)
    diag_ok = key_row <= qry_col

    for p in range(n_pairs):
        q2t = q_ref[0, :, p * LANES:(p + 1) * LANES].astype(F32).T
        qs_ref[p, :, 0:tq] = jnp.where(low, q2t, 0.0).astype(BF16)
        qs_ref[p, :, tq:2 * tq] = jnp.where(low, 0.0, q2t).astype(BF16)
        m_ref[p] = jnp.full((1, 2 * tq), NEG_BIG, F32)
        l_ref[p] = jnp.zeros((1, 2 * tq), F32)
        acc_ref[p] = jnp.zeros((LANES, 2 * tq), F32)

    def step(j, masked):
        start = pl.multiple_of(j * tk, tk)
        cb = c_ref[0, 0, pl.ds(start, tk), :]
        scores = []
        for p in range(n_pairs):
            kb = k_ref[0, pl.ds(start, tk), p * LANES:(p + 1) * LANES]
            scores.append(_dot(kb, qs_ref[p]))
        probs, alphas = [], []
        for p in range(n_pairs):
            s = scores[p]
            s0 = s[:, 0:tq] - cb[:, 2 * p:2 * p + 1]
            s1 = s[:, tq:2 * tq] - cb[:, 2 * p + 1:2 * p + 2]
            if masked:
                s0 = jnp.where(diag_ok, s0, NEG_BIG)
                s1 = jnp.where(diag_ok, s1, NEG_BIG)
            s = jnp.concatenate([s0, s1], axis=1)
            m_prev = m_ref[p]
            m_new = jnp.maximum(m_prev, jnp.max(s, axis=0, keepdims=True))
            alpha = jnp.exp(m_prev - m_new)
            pr = jnp.exp(s - m_new)
            l_ref[p] = alpha * l_ref[p] + jnp.sum(pr, axis=0, keepdims=True)
            m_ref[p] = m_new
            probs.append(pr.astype(BF16))
            alphas.append(alpha)
        for p in range(n_pairs):
            vtb = vt_ref[p * LANES:(p + 1) * LANES, pl.ds(start, tk)]
            acc_ref[p] = alphas[p] * acc_ref[p] + _dot(vtb, probs[p])

    def body(j, carry):
        step(j, False)
        return carry

    lax.fori_loop(0, qi, body, 0)
    step(qi, True)

    for p in range(n_pairs):
        o = acc_ref[p] / l_ref[p]
        o = jnp.where(low, o[:, 0:tq], o[:, tq:2 * tq])
        o_ref[0, :, p * LANES:(p + 1) * LANES] = o.T.astype(o_ref.dtype)


def _fox_attn(q3, k3, vt, c4):
    b, t, d = q3.shape
    width = ATT_PAIRS * LANES
    row_state = pltpu.VMEM((ATT_PAIRS, 1, 2 * ATT_TILE), F32)
    return pl.pallas_call(
        _fox_attn_kernel,
        grid=(b, d // width, t // ATT_TILE),
        in_specs=[
            pl.BlockSpec((1, ATT_TILE, width), lambda bi, g, qi: (bi, qi, g)),
            pl.BlockSpec((1, t, width), lambda bi, g, qi: (bi, 0, g)),
            pl.BlockSpec((width, t), lambda bi, g, qi: (g, bi)),
            pl.BlockSpec((1, 1, t, LANES), lambda bi, g, qi: (bi, g, 0, 0)),
        ],
        out_specs=pl.BlockSpec((1, ATT_TILE, width), lambda bi, g, qi: (bi, qi, g)),
        out_shape=jax.ShapeDtypeStruct((b, t, d), BF16),
        scratch_shapes=[pltpu.VMEM((ATT_PAIRS, LANES, 2 * ATT_TILE), BF16), row_state, row_state,
                        pltpu.VMEM((ATT_PAIRS, LANES, 2 * ATT_TILE), F32)],
        compiler_params=_cparams(("parallel", "parallel", "arbitrary")),
        name="fox_attn",
    )(q3, k3, vt, c4)


def _ffn_kernel(x_ref, g_in_ref, g_out_ref, wup_ref, cw_ref, wdn_ref, o_ref,
                h_ref, ext_ref, prev_ref, acc_ref):
    tm = x_ref.shape[1]
    halo = SUBLANES
    n_chunks = wup_ref.shape[0]
    fc = wdn_ref.shape[1]

    @pl.when(pl.program_id(1) == 0)
    def _():
        prev_ref[...] = jnp.zeros_like(prev_ref)

    x = x_ref[0]
    h_ref[...] = _rms_rows(x, g_in_ref[...]).astype(BF16)
    acc_ref[...] = jnp.zeros_like(acc_ref)

    for j in range(n_chunks):
        ext = ext_ref.at[j % 2]
        up = _dot(h_ref[...], wup_ref[j])
        ext[0:halo, :] = prev_ref[j]
        ext[halo:halo + tm, :] = up
        prev_ref[j] = up[tm - halo:tm]
        cw = cw_ref[j]
        conv = cw[FFN_CONV:FFN_CONV + 1] + cw[FFN_CONV - 1:FFN_CONV] * up
        for s in range(1, FFN_CONV):
            conv = conv + cw[FFN_CONV - 1 - s:FFN_CONV - s] * ext[halo - s:halo - s + tm, :]
        act = (_gelu_tanh(conv[:, :fc]) * conv[:, fc:]).astype(BF16)
        acc_ref[...] += _dot(act, wdn_ref[j])
    o_ref[0] = x + _rms_rows(acc_ref[...], g_out_ref[...])


def _ffn(x3, g_in, g_out, wup_c, cw_c, wdn_c):
    b, t, d = x3.shape
    n_chunks, _, fc2 = wup_c.shape
    fixed2 = lambda bi, ti: (0, 0)
    fixed3 = lambda bi, ti: (0, 0, 0)
    return pl.pallas_call(
        _ffn_kernel,
        grid=(b, t // FFN_TILE),
        in_specs=[
            pl.BlockSpec((1, FFN_TILE, d), lambda bi, ti: (bi, ti, 0)),
            pl.BlockSpec((1, d), fixed2),
            pl.BlockSpec((1, d), fixed2),
            pl.BlockSpec(wup_c.shape, fixed3, pipeline_mode=pl.Buffered(1)),
            pl.BlockSpec(cw_c.shape, fixed3, pipeline_mode=pl.Buffered(1)),
            pl.BlockSpec(wdn_c.shape, fixed3, pipeline_mode=pl.Buffered(1)),
        ],
        out_specs=pl.BlockSpec((1, FFN_TILE, d), lambda bi, ti: (bi, ti, 0)),
        out_shape=jax.ShapeDtypeStruct((b, t, d), F32),
        scratch_shapes=[
            pltpu.VMEM((FFN_TILE, d), BF16),
            pltpu.VMEM((2, FFN_TILE + SUBLANES, fc2), F32),
            pltpu.VMEM((n_chunks, SUBLANES, fc2), F32),
            pltpu.VMEM((FFN_TILE, d), F32),
        ],
        compiler_params=_cparams(("parallel", "arbitrary")),
        name="conv_ffn",
    )(x3, g_in, g_out, wup_c, cw_c, wdn_c)


def _ffn_weights(w_up, conv_w, conv_b, w_down):
    d = w_up.shape[0]
    n_chunks = D_FF // FF_CHUNK
    wg = w_up[:, :D_FF].reshape(d, n_chunks, FF_CHUNK)
    wv = w_up[:, D_FF:].reshape(d, n_chunks, FF_CHUNK)
    wup_c = jnp.concatenate([wg, wv], axis=-1).transpose(1, 0, 2).astype(BF16)
    taps = jnp.concatenate([conv_w, conv_b[None, :]], axis=0)
    taps = jnp.pad(taps, ((0, SUBLANES - taps.shape[0]), (0, 0)))
    tg = taps[:, :D_FF].reshape(SUBLANES, n_chunks, FF_CHUNK)
    tv = taps[:, D_FF:].reshape(SUBLANES, n_chunks, FF_CHUNK)
    cw_c = jnp.concatenate([tg, tv], axis=-1).transpose(1, 0, 2)
    wdn_c = w_down.reshape(n_chunks, FF_CHUNK, d).astype(BF16)
    return wup_c, cw_c, wdn_c


def kernel(x, norm_gains, even_w_in, hgrn_lb_logits, hgrn_norm, rg_conv_w, rg_conv_b, rg_wa, rg_ba,
           rg_wx, rg_bx, rg_lambda, even_w_out, odd_w_in, fox_f_bias, odd_w_out,
           ffn_w_up, ffn_conv_w, ffn_conv_b, ffn_w_down):
    b, t, d = x.shape
    n = b * t
    x2 = x.reshape(n, d)
    for l in range(DEPTH):
        g = norm_gains[l]
        if l % 2 == 0:
            e = l // 2
            proj = _even_inproj(x2, g[0:1], even_w_in[e].astype(BF16))
            proj3 = proj.reshape(b, t, proj.shape[1])
            oa = _hgrn(proj3, hgrn_lb_logits, hgrn_norm[e][None, :], l)
            w_gates = jnp.concatenate(
                [_block_diag_dense(rg_wa[e]), _block_diag_dense(rg_wx[e])], axis=1).astype(BF16)
            b_gates = jnp.concatenate([rg_ba[e], rg_bx[e]])[None, :]
            ob = _rglru(proj3, rg_conv_w[e], rg_conv_b[e][None, :], w_gates, b_gates,
                        rg_lambda[e][None, :])
            w_out = even_w_out[e].astype(BF16)
            x2 = _outproj([oa.reshape(n, A_WIDTH), ob.reshape(n, B_WIDTH)],
                          [w_out[:A_WIDTH], w_out[A_WIDTH:]], x2, g[1:2])
        else:
            o = l // 2
            w_in = odd_w_in[o]
            wqk = w_in[:, :2 * d].astype(BF16)
            wvt = w_in[:, 2 * d:3 * d].T.astype(BF16)
            wf = jnp.pad(w_in[:, 3 * d:], ((0, 0), (0, LANES - C_HEADS))).astype(BF16)
            q, k, vt, f = _odd_inproj(x2, g[0:1], wqk, wvt, wf)
            f_tb = f.reshape(b, t, LANES)[:, :, :C_HEADS].transpose(1, 0, 2).reshape(t, b * C_HEADS)
            bias_row = jnp.tile(fox_f_bias[o], b)[None, :]
            c_tb = _fox_cumsum(f_tb, bias_row)
            group = 2 * ATT_PAIRS
            c4 = c_tb.reshape(t, b, C_HEADS // group, group).transpose(1, 2, 0, 3)
            c4 = jnp.pad(c4, ((0, 0), (0, 0), (0, 0), (0, LANES - group)))
            att = _fox_attn(q.reshape(b, t, d), k.reshape(b, t, d), vt, c4)
            x2 = _outproj([att.reshape(n, d)], [odd_w_out[o].astype(BF16)], x2, g[1:2])
        wup_c, cw_c, wdn_c = _ffn_weights(ffn_w_up[l], ffn_conv_w[l], ffn_conv_b[l], ffn_w_down[l])
        x2 = _ffn(x2.reshape(b, t, d), g[2:3], g[3:4], wup_c, cw_c, wdn_c).reshape(n, d)
    return x2.reshape(b, t, d)
```

```python
import functools

import jax
import jax.numpy as jnp
from jax import lax
from jax.experimental import pallas as pl
from jax.experimental.pallas import tpu as pltpu

F32 = jnp.float32
BF16 = jnp.bfloat16

D_MODEL = 1024
DEPTH = 2
A_HEADS = 4
A_DIM = 128
A_WIDTH = A_HEADS * A_DIM
HGRN_CHUNK = 64
B_WIDTH = D_MODEL - A_WIDTH
B_BLOCKS = 8
B_BLOCK_DIM = B_WIDTH // B_BLOCKS
B_CONV = 4
RG_C = 8.0
C_HEADS = 16
C_HEAD_DIM = D_MODEL // C_HEADS
D_FF = 2816
FFN_CONV = 3
EPS = 1e-6

LANES = 128
SUBLANES = 8
VMEM_LIMIT = 56 * 1024 * 1024

ROW_TILE = 512
SEQ_TILE = 256
FFN_TILE = 512
FF_CHUNK = 256
FFN_ROWS = 128
ATT_TILE = 256
ATT_PAIRS = 4
NEG_BIG = -1e30

LOG2E = 1.4426950408889634
C_PARTS = 3
GELU_C0 = 0.7978845608028654
GELU_C1 = 0.044715


def _cparams(semantics):
    return pltpu.CompilerParams(dimension_semantics=semantics, vmem_limit_bytes=VMEM_LIMIT)


def _rms_rows(x, gain):
    ms = jnp.mean(x * x, axis=-1, keepdims=True)
    return x * lax.rsqrt(ms + EPS) * gain


def _gelu_tanh_x2(x):
    inner = x * (GELU_C0 + (GELU_C0 * GELU_C1) * (x * x))
    return x * (1.0 + jnp.tanh(inner))


def _gelu_tanh(x):
    return 0.5 * _gelu_tanh_x2(x)


def _sigmoid(x):
    return 1.0 / (1.0 + jnp.exp(-x))


def _dot(a, b):
    return jnp.dot(a, b, preferred_element_type=F32)


def _dot_nt(a, b):
    return lax.dot_general(a, b, (((1,), (1,)), ((), ())), preferred_element_type=F32)


def _dot_tn(a, b):
    return lax.dot_general(a, b, (((0,), (0,)), ((), ())), preferred_element_type=F32)


def _split_bf16(x):
    hi = x.astype(BF16)
    lo = (x - hi.astype(F32)).astype(BF16)
    return hi, lo


def _even_inproj_kernel(x_ref, g_ref, w_ref, o_ref, *, col_chunk):
    h = _rms_rows(x_ref[...], g_ref[...]).astype(BF16)
    for c0 in range(0, o_ref.shape[-1], col_chunk):
        o_ref[:, c0:c0 + col_chunk] = _dot(h, w_ref[:, c0:c0 + col_chunk])


def _even_inproj(x2, gain, w_bf16):
    n, d = x2.shape
    n_out = w_bf16.shape[1]
    return pl.pallas_call(
        functools.partial(_even_inproj_kernel, col_chunk=512),
        grid=(n // ROW_TILE,),
        in_specs=[
            pl.BlockSpec((ROW_TILE, d), lambda i: (i, 0)),
            pl.BlockSpec((1, d), lambda i: (0, 0)),
            pl.BlockSpec((d, n_out), lambda i: (0, 0)),
        ],
        out_specs=pl.BlockSpec((ROW_TILE, n_out), lambda i: (i, 0)),
        out_shape=jax.ShapeDtypeStruct((n, n_out), F32),
        compiler_params=_cparams(("parallel",)),
        name="even_inproj",
    )(x2, gain, w_bf16)


def _odd_inproj_kernel(x_ref, g_ref, wqk_ref, wvt_ref, wf_ref, q_ref, k_ref, vt_ref, f_ref):
    h = _rms_rows(x_ref[...], g_ref[...]).astype(BF16)
    d = q_ref.shape[-1]
    q_ref[...] = (_dot(h, wqk_ref[:, 0:d]) * (LOG2E * C_HEAD_DIM ** -0.5)).astype(BF16)
    k_ref[...] = _dot(h, wqk_ref[:, d:2 * d]).astype(BF16)
    vt_ref[...] = _dot_nt(wvt_ref[...], h).astype(BF16)
    f_ref[...] = _dot(h, wf_ref[...])


def _odd_inproj(x2, gain, wqk_bf16, wvt_bf16, wf_bf16):
    n, d = x2.shape
    row = lambda i: (i, 0)
    fixed = lambda i: (0, 0)
    return pl.pallas_call(
        _odd_inproj_kernel,
        grid=(n // ROW_TILE,),
        in_specs=[
            pl.BlockSpec((ROW_TILE, d), row),
            pl.BlockSpec((1, d), fixed),
            pl.BlockSpec((d, 2 * d), fixed),
            pl.BlockSpec((d, d), fixed),
            pl.BlockSpec((d, LANES), fixed),
        ],
        out_specs=[
            pl.BlockSpec((ROW_TILE, d), row),
            pl.BlockSpec((ROW_TILE, d), row),
            pl.BlockSpec((d, ROW_TILE), lambda i: (0, i)),
            pl.BlockSpec((ROW_TILE, LANES), row),
        ],
        out_shape=[
            jax.ShapeDtypeStruct((n, d), BF16),
            jax.ShapeDtypeStruct((n, d), BF16),
            jax.ShapeDtypeStruct((d, n), BF16),
            jax.ShapeDtypeStruct((n, LANES), F32),
        ],
        compiler_params=_cparams(("parallel",)),
        name="odd_inproj",
    )(x2, gain, wqk_bf16, wvt_bf16, wf_bf16)


def _outproj_kernel(*refs, n_lhs):
    a_refs = refs[:n_lhs]
    w_refs = refs[n_lhs:2 * n_lhs]
    x_ref, g_ref, o_ref = refs[2 * n_lhs:]
    mix = _dot(a_refs[0][...].astype(BF16), w_refs[0][...])
    for a_ref, w_ref in zip(a_refs[1:], w_refs[1:]):
        mix = mix + _dot(a_ref[...].astype(BF16), w_ref[...])
    o_ref[...] = x_ref[...] + _rms_rows(mix, g_ref[...])


def _outproj(lhs_list, w_list, x2, gain):
    n, d = x2.shape
    n_lhs = len(lhs_list)
    row = lambda i: (i, 0)
    fixed = lambda i: (0, 0)
    in_specs = [pl.BlockSpec((ROW_TILE, a.shape[1]), row) for a in lhs_list]
    in_specs += [pl.BlockSpec(w.shape, fixed) for w in w_list]
    in_specs += [pl.BlockSpec((ROW_TILE, d), row), pl.BlockSpec((1, d), fixed)]
    return pl.pallas_call(
        functools.partial(_outproj_kernel, n_lhs=n_lhs),
        grid=(n // ROW_TILE,),
        in_specs=in_specs,
        out_specs=pl.BlockSpec((ROW_TILE, d), row),
        out_shape=jax.ShapeDtypeStruct((n, d), F32),
        compiler_params=_cparams(("parallel",)),
        name="outproj",
    )(*lhs_list, *w_list, x2, gain)


def _hgrn_kernel(q_ref, f_ref, i_ref, g_ref, lbz_ref, gain_ref, o_ref, st_ref, *, layer):
    tt = q_ref.shape[1]

    @pl.when(pl.program_id(1) == 0)
    def _():
        st_ref[...] = jnp.zeros_like(st_ref)

    z = lbz_ref[...]
    ez = jnp.exp(z - jnp.max(z, axis=0, keepdims=True))
    lb = jnp.sum(ez[:layer + 1], axis=0, keepdims=True) / jnp.sum(ez, axis=0, keepdims=True)

    forget = lb + (1.0 - lb) * _sigmoid(f_ref[0])
    logf = jnp.log(forget)

    r = lax.broadcasted_iota(jnp.int32, (tt, tt), 0)
    c = lax.broadcasted_iota(jnp.int32, (tt, tt), 1)
    shift = HGRN_CHUNK.bit_length() - 1
    same_chunk = lax.shift_right_logical(r, shift) == lax.shift_right_logical(c, shift)
    tri = jnp.where((c <= r) & same_chunk, 1.0, 0.0).astype(BF16)
    hi, lo = _split_bf16(logf)
    bcum = _dot(tri, hi) + _dot(tri, lo)

    qs = q_ref[0]
    qs = qs * _sigmoid(qs)
    kk = 1.0 - forget
    vv = i_ref[0]
    gate = _sigmoid(g_ref[0])
    gain = gain_ref[...]

    cr = lax.broadcasted_iota(jnp.int32, (HGRN_CHUNK, HGRN_CHUNK), 0)
    cc = lax.broadcasted_iota(jnp.int32, (HGRN_CHUNK, HGRN_CHUNK), 1)
    causal = cc <= cr

    for ch in range(tt // HGRN_CHUNK):
        rows = slice(ch * HGRN_CHUNK, (ch + 1) * HGRN_CHUNK)
        bc = bcum[rows]
        b_last = bc[HGRN_CHUNK - 1:HGRN_CHUNK]
        q_dec = (qs[rows] * jnp.exp(bc)).astype(BF16)
        k_intra = (kk[rows] * jnp.exp(-bc)).astype(BF16)
        k_upd = (kk[rows] * jnp.exp(b_last - bc)).astype(BF16)
        dec = jnp.exp(b_last)
        v_bf = vv[rows].astype(BF16)
        outs = []
        for hd in range(A_HEADS):
            cols = slice(hd * A_DIM, (hd + 1) * A_DIM)
            st = st_ref[hd]
            scores = jnp.where(causal, _dot_nt(q_dec[:, cols], k_intra[:, cols]), 0.0)
            o = _dot(scores.astype(BF16), v_bf[:, cols]) + _dot_nt(q_dec[:, cols], st.astype(BF16))
            st_ref[hd] = dec[:, cols] * st + _dot_tn(v_bf[:, cols], k_upd[:, cols])
            o = o * lax.rsqrt(jnp.mean(o * o, axis=-1, keepdims=True) + EPS)
            outs.append(o)
        o_all = jnp.concatenate(outs, axis=-1)
        o_ref[0, rows, :] = o_all * gain * gate[rows]


def _hgrn(proj3, lb_logits, norm_gain, layer):
    b, t, _ = proj3.shape
    spec = lambda j: pl.BlockSpec((1, SEQ_TILE, A_WIDTH), lambda bi, ti, j=j: (bi, ti, j))
    fixed = lambda bi, ti: (0, 0)
    return pl.pallas_call(
        functools.partial(_hgrn_kernel, layer=layer),
        grid=(b, t // SEQ_TILE),
        in_specs=[spec(0), spec(1), spec(2), spec(3),
                  pl.BlockSpec(lb_logits.shape, fixed),
                  pl.BlockSpec((1, A_WIDTH), fixed)],
        out_specs=pl.BlockSpec((1, SEQ_TILE, A_WIDTH), lambda bi, ti: (bi, ti, 0)),
        out_shape=jax.ShapeDtypeStruct((b, t, A_WIDTH), F32),
        scratch_shapes=[pltpu.VMEM((A_HEADS, A_DIM, A_DIM), F32)],
        compiler_params=_cparams(("parallel", "arbitrary")),
        name="hgrn2",
    )(proj3, proj3, proj3, proj3, lb_logits, norm_gain)


def _rglru_kernel(x_ref, y_ref, cw_ref, cb_ref, wg_ref, bg_ref, lam_ref, o_ref,
                  xext_ref, hprev_ref):
    tt = x_ref.shape[1]
    halo = SUBLANES

    @pl.when(pl.program_id(1) == 0)
    def _():
        xext_ref[0:halo, :] = jnp.zeros((halo, B_WIDTH), F32)
        hprev_ref[...] = jnp.zeros_like(hprev_ref)

    x = x_ref[0]
    xext_ref[halo:halo + tt, :] = x
    cw = cw_ref[...]
    xf = cb_ref[...] + cw[B_CONV - 1:B_CONV] * x
    for j in range(1, B_CONV):
        xf = xf + cw[B_CONV - 1 - j:B_CONV - j] * xext_ref[halo - j:halo - j + tt, :]
    xext_ref[0:halo, :] = x[tt - halo:tt]

    gates = _dot(xf.astype(BF16), wg_ref[...]) + bg_ref[...]
    r = _sigmoid(gates[:, :B_WIDTH])
    i = _sigmoid(gates[:, B_WIDTH:])
    lam = lam_ref[...]
    softplus_neg = jnp.maximum(-lam, 0.0) + jnp.log(1.0 + jnp.exp(-jnp.abs(lam)))
    log_a = (-RG_C) * r * softplus_neg
    a = jnp.exp(log_a)
    u = jnp.sqrt(1.0 - jnp.exp(2.0 * log_a)) * (i * xf)

    groups = tt // SUBLANES
    a3 = a.reshape(groups, SUBLANES, B_WIDTH)
    u3 = u.reshape(groups, SUBLANES, B_WIDTH)
    row = lax.broadcasted_iota(jnp.int32, (groups, SUBLANES, B_WIDTH), 1)
    d = 1
    while d < SUBLANES:
        keep = row >= d
        a_sh = jnp.where(keep, pltpu.roll(a3, d, 1), 1.0)
        u_sh = jnp.where(keep, pltpu.roll(u3, d, 1), 0.0)
        u3 = u3 + a3 * u_sh
        a3 = a3 * a_sh
        d *= 2
    carry = hprev_ref[...]
    hs = []
    for gi in range(groups):
        hg = a3[gi] * carry + u3[gi]
        hs.append(hg)
        carry = hg[SUBLANES - 1:SUBLANES]
    hprev_ref[...] = carry
    h = jnp.concatenate(hs, axis=0)
    o_ref[0] = h * _gelu_tanh(y_ref[0])


def _rglru(proj3, conv_w, conv_b, w_gates, b_gates, lam):
    b, t, _ = proj3.shape
    col0 = (4 * A_WIDTH) // B_WIDTH
    fixed = lambda bi, ti: (0, 0)
    return pl.pallas_call(
        _rglru_kernel,
        grid=(b, t // SEQ_TILE),
        in_specs=[
            pl.BlockSpec((1, SEQ_TILE, B_WIDTH), lambda bi, ti: (bi, ti, col0)),
            pl.BlockSpec((1, SEQ_TILE, B_WIDTH), lambda bi, ti: (bi, ti, col0 + 1)),
            pl.BlockSpec(conv_w.shape, fixed),
            pl.BlockSpec(conv_b.shape, fixed),
            pl.BlockSpec(w_gates.shape, fixed),
            pl.BlockSpec(b_gates.shape, fixed),
            pl.BlockSpec(lam.shape, fixed),
        ],
        out_specs=pl.BlockSpec((1, SEQ_TILE, B_WIDTH), lambda bi, ti: (bi, ti, 0)),
        out_shape=jax.ShapeDtypeStruct((b, t, B_WIDTH), F32),
        scratch_shapes=[pltpu.VMEM((SEQ_TILE + SUBLANES, B_WIDTH), F32),
                        pltpu.VMEM((1, B_WIDTH), F32)],
        compiler_params=_cparams(("parallel", "arbitrary")),
        name="rglru",
    )(proj3, proj3, conv_w, conv_b, w_gates, b_gates, lam)


def _block_diag_dense(w):
    nb, bd, _ = w.shape
    eye = jnp.eye(nb, dtype=w.dtype)
    return jnp.einsum('nij,nm->nimj', w, eye).reshape(nb * bd, nb * bd)


def _fox_cumsum_kernel(f_ref, b_ref, o_ref, *, blk):
    t, w = f_ref.shape
    z = f_ref[...] + b_ref[...]
    logsig = jnp.minimum(z, 0.0) - jnp.log(1.0 + jnp.exp(-jnp.abs(z)))
    r = lax.broadcasted_iota(jnp.int32, (blk, blk), 0)
    c = lax.broadcasted_iota(jnp.int32, (blk, blk), 1)
    tri = jnp.where(c <= r, 1.0, 0.0).astype(BF16)
    carry = jnp.zeros((1, w), F32)
    for s in range(0, t, blk):
        x = logsig[s:s + blk]
        hi = x.astype(BF16)
        r1 = x - hi.astype(F32)
        mid = r1.astype(BF16)
        lo = (r1 - mid.astype(F32)).astype(BF16)
        cs = _dot(tri, hi) + _dot(tri, mid) + _dot(tri, lo) + carry
        o_ref[s:s + blk, :] = cs * (-LOG2E)
        carry = cs[blk - 1:blk]


def _fox_cumsum(f_tb, bias_row):
    t, w = f_tb.shape
    return pl.pallas_call(
        functools.partial(_fox_cumsum_kernel, blk=256),
        out_shape=jax.ShapeDtypeStruct((t, w), F32),
        compiler_params=pltpu.CompilerParams(vmem_limit_bytes=VMEM_LIMIT),
        name="fox_cumsum",
    )(f_tb, bias_row)


def _fox_attn_kernel(q_ref, k_ref, vt_ref, c_ref, o_ref, ka_ref, qs_ref, m_ref, l_ref, acc_ref):
    tq = q_ref.shape[1]
    tk = tq
    n_pairs = q_ref.shape[2] // LANES
    qi = pl.program_id(2)
    dim = lax.broadcasted_iota(jnp.int32, (LANES, tq), 0)
    low = dim < C_HEAD_DIM
    key_row = lax.broadcasted_iota(jnp.int32, (tk, tq), 0)
    qry_col = lax.broadcasted_iota(jnp.int32, (tk, tq), 1)
    diag_ok = key_row <= qry_col

    @pl.when(qi == 0)
    def _():
        rest = c_ref[0, 0]
        pieces = []
        for _ in range(C_PARTS):
            part = rest.astype(BF16)
            pieces.append(part)
            rest = rest - part.astype(F32)
        c_parts = jnp.concatenate(pieces, axis=1)
        src = lax.broadcasted_iota(jnp.int32, (C_PARTS * LANES, LANES), 0)
        dst = lax.broadcasted_iota(jnp.int32, (C_PARTS * LANES, LANES), 1)
        for p in range(n_pairs):
            pick = None
            for hd in range(2):
                for part in range(C_PARTS):
                    hit = (src == part * LANES + 2 * p + hd) & (dst == hd * C_PARTS + part)
                    pick = hit if pick is None else (pick | hit)
            place = jnp.where(pick, 1.0, 0.0).astype(BF16)
            ka_ref[p, :, 0:LANES] = k_ref[0, :, p * LANES:(p + 1) * LANES]
            ka_ref[p, :, LANES:2 * LANES] = _dot(c_parts, place).astype(BF16)

    for p in range(n_pairs):
        q2t = q_ref[0, :, p * LANES:(p + 1) * LANES].astype(F32).T
        qs_ref[p, 0:LANES, 0:tq] = jnp.where(low, q2t, 0.0).astype(BF16)
        qs_ref[p, 0:LANES, tq:2 * tq] = jnp.where(low, 0.0, q2t).astype(BF16)
        qs_ref[p, LANES:2 * LANES, 0:tq] = jnp.where(dim < C_PARTS, 1.0, 0.0).astype(BF16)
        qs_ref[p, LANES:2 * LANES, tq:2 * tq] = jnp.where(
            (dim >= C_PARTS) & (dim < 2 * C_PARTS), 1.0, 0.0).astype(BF16)
        m_ref[p] = jnp.full((1, 2 * tq), NEG_BIG, F32)
        l_ref[p] = jnp.zeros((1, 2 * tq), F32)
        acc_ref[p] = jnp.zeros((LANES, 2 * tq), F32)

    def step(j, masked):
        start = pl.multiple_of(j * tk, tk)
        scores = []
        for p in range(n_pairs):
            scores.append(_dot(ka_ref[p, pl.ds(start, tk), :], qs_ref[p]))
        probs, alphas = [], []
        for p in range(n_pairs):
            s = scores[p]
            if masked:
                s = jnp.concatenate([jnp.where(diag_ok, s[:, 0:tq], NEG_BIG),
                                     jnp.where(diag_ok, s[:, tq:2 * tq], NEG_BIG)], axis=1)
            m_prev = m_ref[p]
            m_new = jnp.maximum(m_prev, jnp.max(s, axis=0, keepdims=True))
            alpha = jnp.exp2(m_prev - m_new)
            pr = jnp.exp2(s - m_new)
            l_ref[p] = alpha * l_ref[p] + jnp.sum(pr, axis=0, keepdims=True)
            m_ref[p] = m_new
            probs.append(pr.astype(BF16))
            alphas.append(alpha)
        for p in range(n_pairs):
            vtb = vt_ref[p * LANES:(p + 1) * LANES, pl.ds(start, tk)]
            acc_ref[p] = alphas[p] * acc_ref[p] + _dot(vtb, probs[p])

    def body(j, carry):
        step(j, False)
        return carry

    lax.fori_loop(0, qi, body, 0)
    step(qi, True)

    for p in range(n_pairs):
        o = acc_ref[p] / l_ref[p]
        o = jnp.where(low, o[:, 0:tq], o[:, tq:2 * tq])
        o_ref[0, :, p * LANES:(p + 1) * LANES] = o.T.astype(o_ref.dtype)


def _fox_attn(q3, k3, vt, c4):
    b, t, d = q3.shape
    width = ATT_PAIRS * LANES
    row_state = pltpu.VMEM((ATT_PAIRS, 1, 2 * ATT_TILE), F32)
    return pl.pallas_call(
        _fox_attn_kernel,
        grid=(b, d // width, t // ATT_TILE),
        in_specs=[
            pl.BlockSpec((1, ATT_TILE, width), lambda bi, g, qi: (bi, qi, g)),
            pl.BlockSpec((1, t, width), lambda bi, g, qi: (bi, 0, g)),
            pl.BlockSpec((width, t), lambda bi, g, qi: (g, bi)),
            pl.BlockSpec((1, 1, t, LANES), lambda bi, g, qi: (bi, g, 0, 0)),
        ],
        out_specs=pl.BlockSpec((1, ATT_TILE, width), lambda bi, g, qi: (bi, qi, g)),
        out_shape=jax.ShapeDtypeStruct((b, t, d), BF16),
        scratch_shapes=[pltpu.VMEM((ATT_PAIRS, t, 2 * LANES), BF16),
                        pltpu.VMEM((ATT_PAIRS, 2 * LANES, 2 * ATT_TILE), BF16), row_state, row_state,
                        pltpu.VMEM((ATT_PAIRS, LANES, 2 * ATT_TILE), F32)],
        compiler_params=_cparams(("parallel", "parallel", "arbitrary")),
        name="fox_attn",
    )(q3, k3, vt, c4)


def _ffn_kernel(x_ref, g_in_ref, g_out_ref, wup_ref, cw_ref, wdn_ref, o_ref,
                h_ref, ext_ref, prev_ref, act_ref, acc_ref):
    tm = x_ref.shape[1]
    halo = SUBLANES
    n_chunks = wup_ref.shape[0]
    fc = wdn_ref.shape[1]

    @pl.when(pl.program_id(1) == 0)
    def _():
        prev_ref[...] = jnp.zeros_like(prev_ref)

    x = x_ref[0]
    h_ref[...] = _rms_rows(x, g_in_ref[...]).astype(BF16)
    acc_ref[...] = jnp.zeros_like(acc_ref)

    def issue_up(j):
        ext = ext_ref.at[j % 2]
        ext[0:halo, :] = prev_ref[j]
        ext[halo:halo + tm, :] = _dot(h_ref[...], wup_ref[j])

    def issue_down(j):
        acc_ref[...] += _dot(act_ref[j % 2], wdn_ref[j])

    issue_up(0)
    for j in range(n_chunks):
        if j + 1 < n_chunks:
            issue_up(j + 1)
        if j >= 1:
            issue_down(j - 1)
        ext = ext_ref.at[j % 2]
        prev_ref[j] = ext[tm:tm + halo, :]
        cw = cw_ref[j]
        for r0 in range(0, tm, FFN_ROWS):
            base = halo + r0
            conv = cw[FFN_CONV:FFN_CONV + 1] + cw[FFN_CONV - 1:FFN_CONV] * ext[base:base + FFN_ROWS, :]
            for s in range(1, FFN_CONV):
                conv = conv + cw[FFN_CONV - 1 - s:FFN_CONV - s] * ext[base - s:base - s + FFN_ROWS, :]
            act_ref[j % 2, r0:r0 + FFN_ROWS, :] = (_gelu_tanh_x2(conv[:, :fc]) * conv[:, fc:]).astype(BF16)
    issue_down(n_chunks - 1)
    o_ref[0] = x + _rms_rows(acc_ref[...], g_out_ref[...])


def _ffn(x3, g_in, g_out, wup_c, cw_c, wdn_c):
    b, t, d = x3.shape
    n_chunks, _, fc2 = wup_c.shape
    fixed2 = lambda bi, ti: (0, 0)
    fixed3 = lambda bi, ti: (0, 0, 0)
    return pl.pallas_call(
        _ffn_kernel,
        grid=(b, t // FFN_TILE),
        in_specs=[
            pl.BlockSpec((1, FFN_TILE, d), lambda bi, ti: (bi, ti, 0)),
            pl.BlockSpec((1, d), fixed2),
            pl.BlockSpec((1, d), fixed2),
            pl.BlockSpec(wup_c.shape, fixed3, pipeline_mode=pl.Buffered(1)),
            pl.BlockSpec(cw_c.shape, fixed3, pipeline_mode=pl.Buffered(1)),
            pl.BlockSpec(wdn_c.shape, fixed3, pipeline_mode=pl.Buffered(1)),
        ],
        out_specs=pl.BlockSpec((1, FFN_TILE, d), lambda bi, ti: (bi, ti, 0)),
        out_shape=jax.ShapeDtypeStruct((b, t, d), F32),
        scratch_shapes=[
            pltpu.VMEM((FFN_TILE, d), BF16),
            pltpu.VMEM((2, FFN_TILE + SUBLANES, fc2), F32),
            pltpu.VMEM((n_chunks, SUBLANES, fc2), F32),
            pltpu.VMEM((2, FFN_TILE, fc2 // 2), BF16),
            pltpu.VMEM((FFN_TILE, d), F32),
        ],
        compiler_params=_cparams(("parallel", "arbitrary")),
        name="conv_ffn",
    )(x3, g_in, g_out, wup_c, cw_c, wdn_c)


def _ffn_weights(w_up, conv_w, conv_b, w_down):
    d = w_up.shape[0]
    n_chunks = D_FF // FF_CHUNK
    wg = w_up[:, :D_FF].reshape(d, n_chunks, FF_CHUNK)
    wv = w_up[:, D_FF:].reshape(d, n_chunks, FF_CHUNK)
    wup_c = jnp.concatenate([wg, wv], axis=-1).transpose(1, 0, 2).astype(BF16)
    taps = jnp.concatenate([conv_w, conv_b[None, :]], axis=0)
    taps = jnp.pad(taps, ((0, SUBLANES - taps.shape[0]), (0, 0)))
    tg = taps[:, :D_FF].reshape(SUBLANES, n_chunks, FF_CHUNK)
    tv = (0.5 * taps[:, D_FF:]).reshape(SUBLANES, n_chunks, FF_CHUNK)
    cw_c = jnp.concatenate([tg, tv], axis=-1).transpose(1, 0, 2)
    wdn_c = w_down.reshape(n_chunks, FF_CHUNK, d).astype(BF16)
    return wup_c, cw_c, wdn_c


def kernel(x, norm_gains, even_w_in, hgrn_lb_logits, hgrn_norm, rg_conv_w, rg_conv_b, rg_wa, rg_ba,
           rg_wx, rg_bx, rg_lambda, even_w_out, odd_w_in, fox_f_bias, odd_w_out,
           ffn_w_up, ffn_conv_w, ffn_conv_b, ffn_w_down):
    b, t, d = x.shape
    n = b * t
    x2 = x.reshape(n, d)
    for l in range(DEPTH):
        g = norm_gains[l]
        if l % 2 == 0:
            e = l // 2
            proj = _even_inproj(x2, g[0:1], even_w_in[e].astype(BF16))
            proj3 = proj.reshape(b, t, proj.shape[1])
            oa = _hgrn(proj3, hgrn_lb_logits, hgrn_norm[e][None, :], l)
            w_gates = jnp.concatenate(
                [_block_diag_dense(rg_wa[e]), _block_diag_dense(rg_wx[e])], axis=1).astype(BF16)
            b_gates = jnp.concatenate([rg_ba[e], rg_bx[e]])[None, :]
            ob = _rglru(proj3, rg_conv_w[e], rg_conv_b[e][None, :], w_gates, b_gates,
                        rg_lambda[e][None, :])
            w_out = even_w_out[e].astype(BF16)
            x2 = _outproj([oa.reshape(n, A_WIDTH), ob.reshape(n, B_WIDTH)],
                          [w_out[:A_WIDTH], w_out[A_WIDTH:]], x2, g[1:2])
        else:
            o = l // 2
            w_in = odd_w_in[o]
            wqk = w_in[:, :2 * d].astype(BF16)
            wvt = w_in[:, 2 * d:3 * d].T.astype(BF16)
            wf = jnp.pad(w_in[:, 3 * d:], ((0, 0), (0, LANES - C_HEADS))).astype(BF16)
            q, k, vt, f = _odd_inproj(x2, g[0:1], wqk, wvt, wf)
            f_tb = f.reshape(b, t, LANES)[:, :, :C_HEADS].transpose(1, 0, 2).reshape(t, b * C_HEADS)
            bias_row = jnp.tile(fox_f_bias[o], b)[None, :]
            c_tb = _fox_cumsum(f_tb, bias_row)
            group = 2 * ATT_PAIRS
            c4 = c_tb.reshape(t, b, C_HEADS // group, group).transpose(1, 2, 0, 3)
            c4 = jnp.pad(c4, ((0, 0), (0, 0), (0, 0), (0, LANES - group)))
            att = _fox_attn(q.reshape(b, t, d), k.reshape(b, t, d), vt, c4)
            x2 = _outproj([att.reshape(n, d)], [odd_w_out[o].astype(BF16)], x2, g[1:2])
        wup_c, cw_c, wdn_c = _ffn_weights(ffn_w_up[l], ffn_conv_w[l], ffn_conv_b[l], ffn_w_down[l])
        x2 = _ffn(x2.reshape(b, t, d), g[2:3], g[3:4], wup_c, cw_c, wdn_c).reshape(n, d)
    return x2.reshape(b, t, d)
```

```python
import functools

import jax
import jax.numpy as jnp
from jax import lax
from jax.experimental import pallas as pl
from jax.experimental.pallas import tpu as pltpu

F32 = jnp.float32
BF16 = jnp.bfloat16

D_MODEL = 1024
DEPTH = 2
A_HEADS = 4
A_DIM = 128
A_WIDTH = A_HEADS * A_DIM
HGRN_CHUNK = 64
B_WIDTH = D_MODEL - A_WIDTH
B_BLOCKS = 8
B_BLOCK_DIM = B_WIDTH // B_BLOCKS
B_CONV = 4
RG_C = 8.0
C_HEADS = 16
C_HEAD_DIM = D_MODEL // C_HEADS
D_FF = 2816
FFN_CONV = 3
EPS = 1e-6

LANES = 128
SUBLANES = 8
VMEM_LIMIT = 56 * 1024 * 1024

ROW_TILE = 512
SEQ_TILE = 256
FFN_TILE = 512
FF_CHUNK = 256
FFN_ROWS = 128
ATT_TILE = 256
ATT_PAIRS = 8
NEG_BIG = -1e30

LOG2E = 1.4426950408889634
C_PARTS = 3
SUM_ROWS = 16
GELU_C0 = 0.7978845608028654
GELU_C1 = 0.044715


def _cparams(semantics):
    return pltpu.CompilerParams(dimension_semantics=semantics, vmem_limit_bytes=VMEM_LIMIT)


def _rms_rows(x, gain):
    ms = jnp.mean(x * x, axis=-1, keepdims=True)
    return x * lax.rsqrt(ms + EPS) * gain


def _gelu_tanh_x2(x):
    inner = x * (GELU_C0 + (GELU_C0 * GELU_C1) * (x * x))
    return x * (1.0 + jnp.tanh(inner))


def _gelu_tanh(x):
    return 0.5 * _gelu_tanh_x2(x)


def _sigmoid(x):
    return 1.0 / (1.0 + jnp.exp(-x))


def _dot(a, b):
    return jnp.dot(a, b, preferred_element_type=F32)


def _dot_nt(a, b):
    return lax.dot_general(a, b, (((1,), (1,)), ((), ())), preferred_element_type=F32)


def _dot_tn(a, b):
    return lax.dot_general(a, b, (((0,), (0,)), ((), ())), preferred_element_type=F32)


def _split_bf16(x):
    hi = x.astype(BF16)
    lo = (x - hi.astype(F32)).astype(BF16)
    return hi, lo


def _odd_inproj_kernel(x_ref, g_ref, wqk_ref, wvt_ref, wf_ref, q_ref, k_ref, vt_ref, f_ref):
    h = _rms_rows(x_ref[...], g_ref[...]).astype(BF16)
    d = q_ref.shape[-1]
    q_ref[...] = (_dot(h, wqk_ref[:, 0:d]) * (LOG2E * C_HEAD_DIM ** -0.5)).astype(BF16)
    k_ref[...] = _dot(h, wqk_ref[:, d:2 * d]).astype(BF16)
    vt_ref[...] = _dot_nt(wvt_ref[...], h).astype(BF16)
    f_ref[...] = _dot(h, wf_ref[...])


def _odd_inproj(x2, gain, wqk_bf16, wvt_bf16, wf_bf16):
    n, d = x2.shape
    row = lambda i: (i, 0)
    fixed = lambda i: (0, 0)
    return pl.pallas_call(
        _odd_inproj_kernel,
        grid=(n // ROW_TILE,),
        in_specs=[
            pl.BlockSpec((ROW_TILE, d), row),
            pl.BlockSpec((1, d), fixed),
            pl.BlockSpec((d, 2 * d), fixed),
            pl.BlockSpec((d, d), fixed),
            pl.BlockSpec((d, LANES), fixed),
        ],
        out_specs=[
            pl.BlockSpec((ROW_TILE, d), row),
            pl.BlockSpec((ROW_TILE, d), row),
            pl.BlockSpec((d, ROW_TILE), lambda i: (0, i)),
            pl.BlockSpec((ROW_TILE, LANES), row),
        ],
        out_shape=[
            jax.ShapeDtypeStruct((n, d), BF16),
            jax.ShapeDtypeStruct((n, d), BF16),
            jax.ShapeDtypeStruct((d, n), BF16),
            jax.ShapeDtypeStruct((n, LANES), F32),
        ],
        compiler_params=_cparams(("parallel",)),
        name="odd_inproj",
    )(x2, gain, wqk_bf16, wvt_bf16, wf_bf16)


def _outproj_kernel(*refs, n_lhs):
    a_refs = refs[:n_lhs]
    w_refs = refs[n_lhs:2 * n_lhs]
    x_ref, g_ref, o_ref = refs[2 * n_lhs:]
    mix = _dot(a_refs[0][...].astype(BF16), w_refs[0][...])
    for a_ref, w_ref in zip(a_refs[1:], w_refs[1:]):
        mix = mix + _dot(a_ref[...].astype(BF16), w_ref[...])
    o_ref[...] = x_ref[...] + _rms_rows(mix, g_ref[...])


def _outproj(lhs_list, w_list, x2, gain):
    n, d = x2.shape
    n_lhs = len(lhs_list)
    row = lambda i: (i, 0)
    fixed = lambda i: (0, 0)
    in_specs = [pl.BlockSpec((ROW_TILE, a.shape[1]), row) for a in lhs_list]
    in_specs += [pl.BlockSpec(w.shape, fixed) for w in w_list]
    in_specs += [pl.BlockSpec((ROW_TILE, d), row), pl.BlockSpec((1, d), fixed)]
    return pl.pallas_call(
        functools.partial(_outproj_kernel, n_lhs=n_lhs),
        grid=(n // ROW_TILE,),
        in_specs=in_specs,
        out_specs=pl.BlockSpec((ROW_TILE, d), row),
        out_shape=jax.ShapeDtypeStruct((n, d), F32),
        compiler_params=_cparams(("parallel",)),
        name="outproj",
    )(*lhs_list, *w_list, x2, gain)


def _hgrn_body(q, f_logit, v, g, lbz_ref, gain_ref, mix_ref, st_ref, *, layer):
    tt = q.shape[0]

    @pl.when(pl.program_id(1) == 0)
    def _():
        st_ref[...] = jnp.zeros_like(st_ref)

    z = lbz_ref[...]
    ez = jnp.exp(z - jnp.max(z, axis=0, keepdims=True))
    lb = jnp.sum(ez[:layer + 1], axis=0, keepdims=True) / jnp.sum(ez, axis=0, keepdims=True)

    forget = lb + (1.0 - lb) * _sigmoid(f_logit)
    logf = jnp.log(forget)

    r = lax.broadcasted_iota(jnp.int32, (tt, tt), 0)
    c = lax.broadcasted_iota(jnp.int32, (tt, tt), 1)
    shift = HGRN_CHUNK.bit_length() - 1
    same_chunk = lax.shift_right_logical(r, shift) == lax.shift_right_logical(c, shift)
    tri = jnp.where((c <= r) & same_chunk, 1.0, 0.0).astype(BF16)
    hi, lo = _split_bf16(logf)
    bcum = _dot(tri, hi) + _dot(tri, lo)

    qs = q.astype(F32)
    qs = qs * _sigmoid(qs)
    kk = 1.0 - forget
    vv = v
    gate = _sigmoid(g.astype(F32))
    gain = gain_ref[...]

    cr = lax.broadcasted_iota(jnp.int32, (HGRN_CHUNK, HGRN_CHUNK), 0)
    cc = lax.broadcasted_iota(jnp.int32, (HGRN_CHUNK, HGRN_CHUNK), 1)
    causal = cc <= cr

    for ch in range(tt // HGRN_CHUNK):
        rows = slice(ch * HGRN_CHUNK, (ch + 1) * HGRN_CHUNK)
        bc = bcum[rows]
        b_last = bc[HGRN_CHUNK - 1:HGRN_CHUNK]
        q_dec = (qs[rows] * jnp.exp(bc)).astype(BF16)
        k_intra = (kk[rows] * jnp.exp(-bc)).astype(BF16)
        k_upd = (kk[rows] * jnp.exp(b_last - bc)).astype(BF16)
        dec = jnp.exp(b_last)
        v_bf = vv[rows]
        outs = []
        for hd in range(A_HEADS):
            cols = slice(hd * A_DIM, (hd + 1) * A_DIM)
            st = st_ref[hd]
            scores = jnp.where(causal, _dot_nt(q_dec[:, cols], k_intra[:, cols]), 0.0)
            o = _dot(scores.astype(BF16), v_bf[:, cols]) + _dot_nt(q_dec[:, cols], st.astype(BF16))
            st_ref[hd] = dec[:, cols] * st + _dot_tn(v_bf[:, cols], k_upd[:, cols])
            o = o * lax.rsqrt(jnp.mean(o * o, axis=-1, keepdims=True) + EPS)
            outs.append(o)
        o_all = jnp.concatenate(outs, axis=-1)
        mix_ref[rows, 0:A_WIDTH] = (o_all * gain * gate[rows]).astype(mix_ref.dtype)


def _rglru_body(x_br, y_br, cw_ref, cb_ref, wg_ref, bg_ref, lam_ref, mix_ref,
                xext_ref, hprev_ref):
    tt = x_br.shape[0]
    halo = SUBLANES

    @pl.when(pl.program_id(1) == 0)
    def _():
        xext_ref[0:halo, :] = jnp.zeros((halo, B_WIDTH), F32)
        hprev_ref[...] = jnp.zeros_like(hprev_ref)

    x = x_br.astype(F32)
    xext_ref[halo:halo + tt, :] = x
    cw = cw_ref[...]
    xf = cb_ref[...] + cw[B_CONV - 1:B_CONV] * x
    for j in range(1, B_CONV):
        xf = xf + cw[B_CONV - 1 - j:B_CONV - j] * xext_ref[halo - j:halo - j + tt, :]
    xext_ref[0:halo, :] = x[tt - halo:tt]

    gates = _dot(xf.astype(BF16), wg_ref[...]) + bg_ref[...]
    r = _sigmoid(gates[:, :B_WIDTH])
    i = _sigmoid(gates[:, B_WIDTH:])
    lam = lam_ref[...]
    softplus_neg = jnp.maximum(-lam, 0.0) + jnp.log(1.0 + jnp.exp(-jnp.abs(lam)))
    log_a = (-RG_C) * r * softplus_neg
    a = jnp.exp(log_a)
    u = jnp.sqrt(1.0 - jnp.exp(2.0 * log_a)) * (i * xf)

    groups = tt // SUBLANES
    a3 = a.reshape(groups, SUBLANES, B_WIDTH)
    u3 = u.reshape(groups, SUBLANES, B_WIDTH)
    row = lax.broadcasted_iota(jnp.int32, (groups, SUBLANES, B_WIDTH), 1)
    d = 1
    while d < SUBLANES:
        keep = row >= d
        a_sh = jnp.where(keep, pltpu.roll(a3, d, 1), 1.0)
        u_sh = jnp.where(keep, pltpu.roll(u3, d, 1), 0.0)
        u3 = u3 + a3 * u_sh
        a3 = a3 * a_sh
        d *= 2
    carry = hprev_ref[...]
    hs = []
    for gi in range(groups):
        hg = a3[gi] * carry + u3[gi]
        hs.append(hg)
        carry = hg[SUBLANES - 1:SUBLANES]
    hprev_ref[...] = carry
    h = jnp.concatenate(hs, axis=0)
    mix_ref[:, A_WIDTH:D_MODEL] = (h * _gelu_tanh(y_br.astype(F32))).astype(mix_ref.dtype)


def _even_mixer_kernel(x_ref, gin_ref, wf_ref, win_ref, lbz_ref, gain_ref,
                       cw_ref, cb_ref, wg_ref, bg_ref, lam_ref, wout_ref, gout_ref,
                       o_ref, mix_ref, st_ref, xext_ref, hprev_ref, *, layer):
    x = x_ref[0]
    h = _rms_rows(x, gin_ref[...]).astype(BF16)
    f_logit = _dot(h, wf_ref[...])
    q, v, g, x_br, y_br = [
        _dot(h, win_ref[:, blk * A_WIDTH:(blk + 1) * A_WIDTH]).astype(BF16)
        for blk in range(win_ref.shape[1] // A_WIDTH)]
    _hgrn_body(q, f_logit, v, g, lbz_ref, gain_ref, mix_ref, st_ref, layer=layer)
    _rglru_body(x_br, y_br, cw_ref, cb_ref, wg_ref, bg_ref, lam_ref, mix_ref, xext_ref, hprev_ref)
    mixed = _dot(mix_ref[...], wout_ref[...])
    o_ref[0] = x + _rms_rows(mixed, gout_ref[...])


def _even_mixer(x3, g_in, w_f, w_main, lb_logits, norm_gain, conv_w, conv_b, w_gates, b_gates,
                lam, w_out, g_out, layer):
    b, t, d = x3.shape
    fixed = lambda bi, ti: (0, 0)
    full = lambda a: pl.BlockSpec(a.shape, fixed)
    resident = lambda a: pl.BlockSpec(a.shape, fixed, pipeline_mode=pl.Buffered(1))
    row_tile = pl.BlockSpec((1, SEQ_TILE, d), lambda bi, ti: (bi, ti, 0))
    return pl.pallas_call(
        functools.partial(_even_mixer_kernel, layer=layer),
        grid=(b, t // SEQ_TILE),
        in_specs=[row_tile, full(g_in), resident(w_f), resident(w_main),
                  full(lb_logits), full(norm_gain), full(conv_w), full(conv_b),
                  resident(w_gates), full(b_gates), full(lam), resident(w_out), full(g_out)],
        out_specs=row_tile,
        out_shape=jax.ShapeDtypeStruct((b, t, d), F32),
        scratch_shapes=[pltpu.VMEM((SEQ_TILE, d), BF16),
                        pltpu.VMEM((A_HEADS, A_DIM, A_DIM), F32),
                        pltpu.VMEM((SEQ_TILE + SUBLANES, B_WIDTH), F32),
                        pltpu.VMEM((1, B_WIDTH), F32)],
        compiler_params=_cparams(("parallel", "arbitrary")),
        name="even_mixer",
    )(x3, g_in, w_f, w_main, lb_logits, norm_gain, conv_w, conv_b, w_gates, b_gates, lam,
      w_out, g_out)


def _block_diag_dense(w):
    nb, bd, _ = w.shape
    eye = jnp.eye(nb, dtype=w.dtype)
    return jnp.einsum('nij,nm->nimj', w, eye).reshape(nb * bd, nb * bd)


def _fox_cumsum_kernel(f_ref, b_ref, o_ref, *, blk):
    t, w = f_ref.shape
    z = f_ref[...] + b_ref[...]
    logsig = jnp.minimum(z, 0.0) - jnp.log(1.0 + jnp.exp(-jnp.abs(z)))
    r = lax.broadcasted_iota(jnp.int32, (blk, blk), 0)
    c = lax.broadcasted_iota(jnp.int32, (blk, blk), 1)
    tri = jnp.where(c <= r, 1.0, 0.0).astype(BF16)
    carry = jnp.zeros((1, w), F32)
    for s in range(0, t, blk):
        x = logsig[s:s + blk]
        hi = x.astype(BF16)
        r1 = x - hi.astype(F32)
        mid = r1.astype(BF16)
        lo = (r1 - mid.astype(F32)).astype(BF16)
        cs = _dot(tri, hi) + _dot(tri, mid) + _dot(tri, lo) + carry
        o_ref[s:s + blk, :] = cs * (-LOG2E)
        carry = cs[blk - 1:blk]


def _fox_cumsum(f_tb, bias_row):
    t, w = f_tb.shape
    return pl.pallas_call(
        functools.partial(_fox_cumsum_kernel, blk=256),
        out_shape=jax.ShapeDtypeStruct((t, w), F32),
        compiler_params=pltpu.CompilerParams(vmem_limit_bytes=VMEM_LIMIT),
        name="fox_cumsum",
    )(f_tb, bias_row)


def _fox_attn_kernel(q_ref, k_ref, vt_ref, c_ref, o_ref, ka_ref, qs_ref, m_ref, acc_ref):
    tq = q_ref.shape[1]
    tk = tq
    n_pairs = q_ref.shape[2] // LANES
    qi = pl.program_id(2)
    dim = lax.broadcasted_iota(jnp.int32, (LANES, tq), 0)
    low = dim < C_HEAD_DIM
    key_row = lax.broadcasted_iota(jnp.int32, (tk, tq), 0)
    qry_col = lax.broadcasted_iota(jnp.int32, (tk, tq), 1)
    diag_ok = key_row <= qry_col

    @pl.when(qi == 0)
    def _():
        rest = c_ref[0, 0]
        pieces = []
        for _ in range(C_PARTS):
            part = rest.astype(BF16)
            pieces.append(part)
            rest = rest - part.astype(F32)
        c_parts = jnp.concatenate(pieces, axis=1)
        src = lax.broadcasted_iota(jnp.int32, (C_PARTS * LANES, LANES), 0)
        dst = lax.broadcasted_iota(jnp.int32, (C_PARTS * LANES, LANES), 1)
        for p in range(n_pairs):
            pick = None
            for hd in range(2):
                for part in range(C_PARTS):
                    hit = (src == part * LANES + 2 * p + hd) & (dst == hd * C_PARTS + part)
                    pick = hit if pick is None else (pick | hit)
            place = jnp.where(pick, 1.0, 0.0).astype(BF16)
            ka_ref[p, :, 0:LANES] = k_ref[0, :, p * LANES:(p + 1) * LANES]
            ka_ref[p, :, LANES:2 * LANES] = _dot(c_parts, place).astype(BF16)

    for p in range(n_pairs):
        q2t = q_ref[0, :, p * LANES:(p + 1) * LANES].astype(F32).T
        qs_ref[p, 0:LANES, 0:tq] = jnp.where(low, q2t, 0.0).astype(BF16)
        qs_ref[p, 0:LANES, tq:2 * tq] = jnp.where(low, 0.0, q2t).astype(BF16)
        qs_ref[p, LANES:2 * LANES, 0:tq] = jnp.where(dim < C_PARTS, 1.0, 0.0).astype(BF16)
        qs_ref[p, LANES:2 * LANES, tq:2 * tq] = jnp.where(
            (dim >= C_PARTS) & (dim < 2 * C_PARTS), 1.0, 0.0).astype(BF16)
        m_ref[p] = jnp.full((1, 2 * tq), NEG_BIG, F32)
        acc_ref[p] = jnp.zeros((LANES + SUM_ROWS, 2 * tq), F32)

    ones = jnp.ones((SUM_ROWS, tk), BF16)

    def step(j, masked):
        start = pl.multiple_of(j * tk, tk)
        scores = []
        for p in range(n_pairs):
            scores.append(_dot(ka_ref[p, pl.ds(start, tk), :], qs_ref[p]))
        probs, alphas = [], []
        for p in range(n_pairs):
            s = scores[p]
            if masked:
                s = jnp.concatenate([jnp.where(diag_ok, s[:, 0:tq], NEG_BIG),
                                     jnp.where(diag_ok, s[:, tq:2 * tq], NEG_BIG)], axis=1)
            m_prev = m_ref[p]
            m_new = jnp.maximum(m_prev, jnp.max(s, axis=0, keepdims=True))
            alpha = jnp.exp2(m_prev - m_new)
            pr = jnp.exp2(s - m_new)
            m_ref[p] = m_new
            probs.append(pr.astype(BF16))
            alphas.append(alpha)
        for p in range(n_pairs):
            vtb = jnp.concatenate([vt_ref[p * LANES:(p + 1) * LANES, pl.ds(start, tk)], ones], axis=0)
            acc_ref[p] = alphas[p] * acc_ref[p] + _dot(vtb, probs[p])

    def body(j, carry):
        step(j, False)
        return carry

    lax.fori_loop(0, qi, body, 0)
    step(qi, True)

    for p in range(n_pairs):
        o = acc_ref[p, 0:LANES, :] / acc_ref[p, LANES:LANES + 1, :]
        o = jnp.where(low, o[:, 0:tq], o[:, tq:2 * tq])
        o_ref[0, :, p * LANES:(p + 1) * LANES] = o.T.astype(o_ref.dtype)


def _fox_attn(q3, k3, vt, c4):
    b, t, d = q3.shape
    width = ATT_PAIRS * LANES
    row_state = pltpu.VMEM((ATT_PAIRS, 1, 2 * ATT_TILE), F32)
    return pl.pallas_call(
        _fox_attn_kernel,
        grid=(b, d // width, t // ATT_TILE),
        in_specs=[
            pl.BlockSpec((1, ATT_TILE, width), lambda bi, g, qi: (bi, qi, g)),
            pl.BlockSpec((1, t, width), lambda bi, g, qi: (bi, 0, g)),
            pl.BlockSpec((width, t), lambda bi, g, qi: (g, bi)),
            pl.BlockSpec((1, 1, t, LANES), lambda bi, g, qi: (bi, g, 0, 0)),
        ],
        out_specs=pl.BlockSpec((1, ATT_TILE, width), lambda bi, g, qi: (bi, qi, g)),
        out_shape=jax.ShapeDtypeStruct((b, t, d), BF16),
        scratch_shapes=[pltpu.VMEM((ATT_PAIRS, t, 2 * LANES), BF16),
                        pltpu.VMEM((ATT_PAIRS, 2 * LANES, 2 * ATT_TILE), BF16), row_state,
                        pltpu.VMEM((ATT_PAIRS, LANES + SUM_ROWS, 2 * ATT_TILE), F32)],
        compiler_params=_cparams(("parallel", "parallel", "arbitrary")),
        name="fox_attn",
    )(q3, k3, vt, c4)


def _ffn_kernel(x_ref, g_in_ref, g_out_ref, wup_ref, cw_ref, wdn_ref, o_ref,
                h_ref, ext_ref, prev_ref, act_ref, acc_ref):
    tm = x_ref.shape[1]
    halo = SUBLANES
    n_chunks = wup_ref.shape[0]
    fc = wdn_ref.shape[1]

    @pl.when(pl.program_id(1) == 0)
    def _():
        prev_ref[...] = jnp.zeros_like(prev_ref)

    x = x_ref[0]
    h_ref[...] = _rms_rows(x, g_in_ref[...]).astype(BF16)
    acc_ref[...] = jnp.zeros_like(acc_ref)

    def issue_up(j):
        ext = ext_ref.at[j % 2]
        ext[0:halo, :] = prev_ref[j]
        for c0 in (0, fc):
            ext[halo:halo + tm, c0:c0 + fc] = _dot(h_ref[...], wup_ref[j, :, c0:c0 + fc])

    def issue_down(j):
        acc_ref[...] += _dot(act_ref[j % 2], wdn_ref[j])

    issue_up(0)
    for j in range(n_chunks):
        if j + 1 < n_chunks:
            issue_up(j + 1)
        if j >= 1:
            issue_down(j - 1)
        ext = ext_ref.at[j % 2]
        prev_ref[j] = ext[tm:tm + halo, :]
        cw = cw_ref[j]
        for r0 in range(0, tm, FFN_ROWS):
            base = halo + r0
            conv = cw[FFN_CONV:FFN_CONV + 1] + cw[FFN_CONV - 1:FFN_CONV] * ext[base:base + FFN_ROWS, :]
            for s in range(1, FFN_CONV):
                conv = conv + cw[FFN_CONV - 1 - s:FFN_CONV - s] * ext[base - s:base - s + FFN_ROWS, :]
            act_ref[j % 2, r0:r0 + FFN_ROWS, :] = (_gelu_tanh_x2(conv[:, :fc]) * conv[:, fc:]).astype(BF16)
    issue_down(n_chunks - 1)
    o_ref[0] = x + _rms_rows(acc_ref[...], g_out_ref[...])


def _ffn(x3, g_in, g_out, wup_c, cw_c, wdn_c):
    b, t, d = x3.shape
    n_chunks, _, fc2 = wup_c.shape
    fixed2 = lambda bi, ti: (0, 0)
    fixed3 = lambda bi, ti: (0, 0, 0)
    return pl.pallas_call(
        _ffn_kernel,
        grid=(b, t // FFN_TILE),
        in_specs=[
            pl.BlockSpec((1, FFN_TILE, d), lambda bi, ti: (bi, ti, 0)),
            pl.BlockSpec((1, d), fixed2),
            pl.BlockSpec((1, d), fixed2),
            pl.BlockSpec(wup_c.shape, fixed3, pipeline_mode=pl.Buffered(1)),
            pl.BlockSpec(cw_c.shape, fixed3, pipeline_mode=pl.Buffered(1)),
            pl.BlockSpec(wdn_c.shape, fixed3, pipeline_mode=pl.Buffered(1)),
        ],
        out_specs=pl.BlockSpec((1, FFN_TILE, d), lambda bi, ti: (bi, ti, 0)),
        out_shape=jax.ShapeDtypeStruct((b, t, d), F32),
        scratch_shapes=[
            pltpu.VMEM((FFN_TILE, d), BF16),
            pltpu.VMEM((2, FFN_TILE + SUBLANES, fc2), F32),
            pltpu.VMEM((n_chunks, SUBLANES, fc2), F32),
            pltpu.VMEM((2, FFN_TILE, fc2 // 2), BF16),
            pltpu.VMEM((FFN_TILE, d), F32),
        ],
        compiler_params=_cparams(("parallel", "arbitrary")),
        name="conv_ffn",
    )(x3, g_in, g_out, wup_c, cw_c, wdn_c)


def _ffn_weights(w_up, conv_w, conv_b, w_down):
    d = w_up.shape[0]
    n_chunks = D_FF // FF_CHUNK
    wg = w_up[:, :D_FF].reshape(d, n_chunks, FF_CHUNK)
    wv = w_up[:, D_FF:].reshape(d, n_chunks, FF_CHUNK)
    wup_c = jnp.concatenate([wg, wv], axis=-1).transpose(1, 0, 2).astype(BF16)
    taps = jnp.concatenate([conv_w, conv_b[None, :]], axis=0)
    taps = jnp.pad(taps, ((0, SUBLANES - taps.shape[0]), (0, 0)))
    tg = taps[:, :D_FF].reshape(SUBLANES, n_chunks, FF_CHUNK)
    tv = (0.5 * taps[:, D_FF:]).reshape(SUBLANES, n_chunks, FF_CHUNK)
    cw_c = jnp.concatenate([tg, tv], axis=-1).transpose(1, 0, 2)
    wdn_c = w_down.reshape(n_chunks, FF_CHUNK, d).astype(BF16)
    return wup_c, cw_c, wdn_c


def kernel(x, norm_gains, even_w_in, hgrn_lb_logits, hgrn_norm, rg_conv_w, rg_conv_b, rg_wa, rg_ba,
           rg_wx, rg_bx, rg_lambda, even_w_out, odd_w_in, fox_f_bias, odd_w_out,
           ffn_w_up, ffn_conv_w, ffn_conv_b, ffn_w_down):
    b, t, d = x.shape
    n = b * t
    x2 = x.reshape(n, d)
    for l in range(DEPTH):
        g = norm_gains[l]
        if l % 2 == 0:
            e = l // 2
            w_in = even_w_in[e]
            a2 = 2 * A_WIDTH
            w_main = jnp.concatenate([w_in[:, :A_WIDTH], w_in[:, a2:]], axis=1).astype(BF16)
            w_gates = jnp.concatenate(
                [_block_diag_dense(rg_wa[e]), _block_diag_dense(rg_wx[e])], axis=1).astype(BF16)
            b_gates = jnp.concatenate([rg_ba[e], rg_bx[e]])[None, :]
            x2 = _even_mixer(x2.reshape(b, t, d), g[0:1], w_in[:, A_WIDTH:a2].astype(BF16), w_main,
                             hgrn_lb_logits, hgrn_norm[e][None, :],
                             rg_conv_w[e], rg_conv_b[e][None, :], w_gates, b_gates,
                             rg_lambda[e][None, :], even_w_out[e].astype(BF16), g[1:2], l).reshape(n, d)
        else:
            o = l // 2
            w_in = odd_w_in[o]
            wqk = w_in[:, :2 * d].astype(BF16)
            wvt = w_in[:, 2 * d:3 * d].T.astype(BF16)
            wf = jnp.pad(w_in[:, 3 * d:], ((0, 0), (0, LANES - C_HEADS))).astype(BF16)
            q, k, vt, f = _odd_inproj(x2, g[0:1], wqk, wvt, wf)
            f_tb = f.reshape(b, t, LANES)[:, :, :C_HEADS].transpose(1, 0, 2).reshape(t, b * C_HEADS)
            bias_row = jnp.tile(fox_f_bias[o], b)[None, :]
            c_tb = _fox_cumsum(f_tb, bias_row)
            group = 2 * ATT_PAIRS
            c4 = c_tb.reshape(t, b, C_HEADS // group, group).transpose(1, 2, 0, 3)
            c4 = jnp.pad(c4, ((0, 0), (0, 0), (0, 0), (0, LANES - group)))
            att = _fox_attn(q.reshape(b, t, d), k.reshape(b, t, d), vt, c4)
            x2 = _outproj([att.reshape(n, d)], [odd_w_out[o].astype(BF16)], x2, g[1:2])
        wup_c, cw_c, wdn_c = _ffn_weights(ffn_w_up[l], ffn_conv_w[l], ffn_conv_b[l], ffn_w_down[l])
        x2 = _ffn(x2.reshape(b, t, d), g[2:3], g[3:4], wup_c, cw_c, wdn_c).reshape(n, d)
    return x2.reshape(b, t, d)
```

```python
import functools

import jax
import jax.numpy as jnp
from jax import lax
from jax.experimental import pallas as pl
from jax.experimental.pallas import tpu as pltpu

F32 = jnp.float32
BF16 = jnp.bfloat16

D_MODEL = 1024
DEPTH = 2
A_HEADS = 4
A_DIM = 128
A_WIDTH = A_HEADS * A_DIM
HGRN_CHUNK = 64
B_WIDTH = D_MODEL - A_WIDTH
B_BLOCKS = 8
B_BLOCK_DIM = B_WIDTH // B_BLOCKS
B_CONV = 4
RG_C = 8.0
C_HEADS = 16
C_HEAD_DIM = D_MODEL // C_HEADS
D_FF = 2816
FFN_CONV = 3
EPS = 1e-6

LANES = 128
SUBLANES = 8
VMEM_LIMIT = 56 * 1024 * 1024

ROW_TILE = 512
SEQ_TILE = 512
MIX_SUB = 256
RG_ROWS = 64
FFN_TILE = 512
FF_CHUNK = 256
FFN_ROWS = 128
ATT_TILE = 256
ATT_PAIRS = 8
NEG_BIG = -1e30

LOG2E = 1.4426950408889634
C_PARTS = 3
SUM_ROWS = 16
GELU_C0 = 0.7978845608028654
GELU_C1 = 0.044715


def _cparams(semantics):
    return pltpu.CompilerParams(dimension_semantics=semantics, vmem_limit_bytes=VMEM_LIMIT)


def _rms_rows(x, gain):
    ms = jnp.mean(x * x, axis=-1, keepdims=True)
    return x * lax.rsqrt(ms + EPS) * gain


def _gelu_tanh_x2(x):
    inner = x * (GELU_C0 + (GELU_C0 * GELU_C1) * (x * x))
    return x * (1.0 + jnp.tanh(inner))


def _gelu_tanh(x):
    return 0.5 * _gelu_tanh_x2(x)


def _sigmoid(x):
    return 1.0 / (1.0 + jnp.exp(-x))


def _dot(a, b):
    return jnp.dot(a, b, preferred_element_type=F32)


def _dot_nt(a, b):
    return lax.dot_general(a, b, (((1,), (1,)), ((), ())), preferred_element_type=F32)


def _dot_tn(a, b):
    return lax.dot_general(a, b, (((0,), (0,)), ((), ())), preferred_element_type=F32)


def _interleave(*streams):
    live = list(streams)
    while live:
        for stream in list(live):
            if next(stream, _DONE) is _DONE:
                live.remove(stream)


_DONE = object()


def _split_bf16(x):
    hi = x.astype(BF16)
    lo = (x - hi.astype(F32)).astype(BF16)
    return hi, lo


def _odd_inproj_kernel(x_ref, g_ref, wqt_ref, wk_ref, wvt_ref, wf_ref, qt_ref, k_ref, vt_ref, f_ref):
    h = _rms_rows(x_ref[...], g_ref[...]).astype(BF16)
    qt_ref[...] = (_dot_nt(wqt_ref[...], h) * (LOG2E * C_HEAD_DIM ** -0.5)).astype(BF16)
    k_ref[...] = _dot(h, wk_ref[...]).astype(BF16)
    vt_ref[...] = _dot_nt(wvt_ref[...], h).astype(BF16)
    f_ref[...] = _dot(h, wf_ref[...])


def _odd_inproj(x2, gain, wqt_bf16, wk_bf16, wvt_bf16, wf_bf16):
    n, d = x2.shape
    row = lambda i: (i, 0)
    col = lambda i: (0, i)
    fixed = lambda i: (0, 0)
    return pl.pallas_call(
        _odd_inproj_kernel,
        grid=(n // ROW_TILE,),
        in_specs=[
            pl.BlockSpec((ROW_TILE, d), row),
            pl.BlockSpec((1, d), fixed),
            pl.BlockSpec((d, d), fixed),
            pl.BlockSpec((d, d), fixed),
            pl.BlockSpec((d, d), fixed),
            pl.BlockSpec((d, LANES), fixed),
        ],
        out_specs=[
            pl.BlockSpec((d, ROW_TILE), col),
            pl.BlockSpec((ROW_TILE, d), row),
            pl.BlockSpec((d, ROW_TILE), col),
            pl.BlockSpec((ROW_TILE, LANES), row),
        ],
        out_shape=[
            jax.ShapeDtypeStruct((d, n), BF16),
            jax.ShapeDtypeStruct((n, d), BF16),
            jax.ShapeDtypeStruct((d, n), BF16),
            jax.ShapeDtypeStruct((n, LANES), F32),
        ],
        compiler_params=_cparams(("parallel",)),
        name="odd_inproj",
    )(x2, gain, wqt_bf16, wk_bf16, wvt_bf16, wf_bf16)


def _outproj_kernel(*refs, n_lhs):
    a_refs = refs[:n_lhs]
    w_refs = refs[n_lhs:2 * n_lhs]
    x_ref, g_ref, o_ref = refs[2 * n_lhs:]
    mix = _dot(a_refs[0][...].astype(BF16), w_refs[0][...])
    for a_ref, w_ref in zip(a_refs[1:], w_refs[1:]):
        mix = mix + _dot(a_ref[...].astype(BF16), w_ref[...])
    o_ref[...] = x_ref[...] + _rms_rows(mix, g_ref[...])


def _outproj(lhs_list, w_list, x2, gain):
    n, d = x2.shape
    n_lhs = len(lhs_list)
    row = lambda i: (i, 0)
    fixed = lambda i: (0, 0)
    in_specs = [pl.BlockSpec((ROW_TILE, a.shape[1]), row) for a in lhs_list]
    in_specs += [pl.BlockSpec(w.shape, fixed) for w in w_list]
    in_specs += [pl.BlockSpec((ROW_TILE, d), row), pl.BlockSpec((1, d), fixed)]
    return pl.pallas_call(
        functools.partial(_outproj_kernel, n_lhs=n_lhs),
        grid=(n // ROW_TILE,),
        in_specs=in_specs,
        out_specs=pl.BlockSpec((ROW_TILE, d), row),
        out_shape=jax.ShapeDtypeStruct((n, d), F32),
        compiler_params=_cparams(("parallel",)),
        name="outproj",
    )(*lhs_list, *w_list, x2, gain)


def _hgrn_body(q, f_logit, v, g, lbz_ref, gain_ref, mix_ref, st_ref, *, layer):
    tt = q.shape[0]

    z = lbz_ref[...]
    ez = jnp.exp(z - jnp.max(z, axis=0, keepdims=True))
    lb = jnp.sum(ez[:layer + 1], axis=0, keepdims=True) / jnp.sum(ez, axis=0, keepdims=True)

    forget = lb + (1.0 - lb) * _sigmoid(f_logit)
    logf = jnp.log(forget)

    r = lax.broadcasted_iota(jnp.int32, (tt, tt), 0)
    c = lax.broadcasted_iota(jnp.int32, (tt, tt), 1)
    shift = HGRN_CHUNK.bit_length() - 1
    same_chunk = lax.shift_right_logical(r, shift) == lax.shift_right_logical(c, shift)
    tri = jnp.where((c <= r) & same_chunk, 1.0, 0.0).astype(BF16)
    hi, lo = _split_bf16(logf)
    bcum = _dot(tri, hi) + _dot(tri, lo)

    qs = q.astype(F32)
    qs = qs * _sigmoid(qs)
    kk = 1.0 - forget
    vv = v
    gate = _sigmoid(g.astype(F32))
    gain = gain_ref[...]

    cr = lax.broadcasted_iota(jnp.int32, (HGRN_CHUNK, HGRN_CHUNK), 0)
    cc = lax.broadcasted_iota(jnp.int32, (HGRN_CHUNK, HGRN_CHUNK), 1)
    causal = cc <= cr
    yield

    for ch in range(tt // HGRN_CHUNK):
        rows = slice(ch * HGRN_CHUNK, (ch + 1) * HGRN_CHUNK)
        bc = bcum[rows]
        b_last = bc[HGRN_CHUNK - 1:HGRN_CHUNK]
        q_dec = (qs[rows] * jnp.exp(bc)).astype(BF16)
        k_intra = (kk[rows] * jnp.exp(-bc)).astype(BF16)
        k_upd = (kk[rows] * jnp.exp(b_last - bc)).astype(BF16)
        dec = jnp.exp(b_last)
        v_bf = vv[rows]
        outs = []
        for hd in range(A_HEADS):
            cols = slice(hd * A_DIM, (hd + 1) * A_DIM)
            st = st_ref[hd]
            scores = jnp.where(causal, _dot_nt(q_dec[:, cols], k_intra[:, cols]), 0.0)
            o = _dot(scores.astype(BF16), v_bf[:, cols]) + _dot_nt(q_dec[:, cols], st.astype(BF16))
            st_ref[hd] = dec[:, cols] * st + _dot_tn(v_bf[:, cols], k_upd[:, cols])
            o = o * lax.rsqrt(jnp.mean(o * o, axis=-1, keepdims=True) + EPS)
            outs.append(o)
        o_all = jnp.concatenate(outs, axis=-1)
        mix_ref[rows, 0:A_WIDTH] = (o_all * gain * gate[rows]).astype(mix_ref.dtype)
        yield


def _rglru_body(x_br, y_br, cw_ref, cb_ref, wg_ref, bg_ref, lam_ref, mix_ref,
                xext_ref, hprev_ref):
    tt = x_br.shape[0]
    halo = SUBLANES
    x = x_br.astype(F32)
    xext_ref[halo:halo + tt, :] = x
    cw = cw_ref[...]
    xf = cb_ref[...] + cw[B_CONV - 1:B_CONV] * x
    for j in range(1, B_CONV):
        xf = xf + cw[B_CONV - 1 - j:B_CONV - j] * xext_ref[halo - j:halo - j + tt, :]
    xext_ref[0:halo, :] = x[tt - halo:tt]

    gates = _dot(xf.astype(BF16), wg_ref[...]) + bg_ref[...]
    lam = lam_ref[...]
    softplus_neg = jnp.maximum(-lam, 0.0) + jnp.log(1.0 + jnp.exp(-jnp.abs(lam)))
    yield

    groups = RG_ROWS // SUBLANES
    row = lax.broadcasted_iota(jnp.int32, (groups, SUBLANES, B_WIDTH), 1)
    carry = hprev_ref[...]
    for r0 in range(0, tt, RG_ROWS):
        rows = slice(r0, r0 + RG_ROWS)
        r = _sigmoid(gates[rows, :B_WIDTH])
        i = _sigmoid(gates[rows, B_WIDTH:])
        log_a = (-RG_C) * r * softplus_neg
        a = jnp.exp(log_a)
        u = jnp.sqrt(1.0 - jnp.exp(2.0 * log_a)) * (i * xf[rows])
        a3 = a.reshape(groups, SUBLANES, B_WIDTH)
        u3 = u.reshape(groups, SUBLANES, B_WIDTH)
        d = 1
        while d < SUBLANES:
            keep = row >= d
            a_sh = jnp.where(keep, pltpu.roll(a3, d, 1), 1.0)
            u_sh = jnp.where(keep, pltpu.roll(u3, d, 1), 0.0)
            u3 = u3 + a3 * u_sh
            a3 = a3 * a_sh
            d *= 2
        hs = []
        for gi in range(groups):
            hg = a3[gi] * carry + u3[gi]
            hs.append(hg)
            carry = hg[SUBLANES - 1:SUBLANES]
        h = jnp.concatenate(hs, axis=0)
        mix_ref[rows, A_WIDTH:D_MODEL] = (h * _gelu_tanh(y_br[rows].astype(F32))).astype(mix_ref.dtype)
        yield
    hprev_ref[...] = carry


def _even_mixer_kernel(x_ref, gin_ref, wf_ref, win_ref, lbz_ref, gain_ref,
                       cw_ref, cb_ref, wg_ref, bg_ref, lam_ref, wout_ref, gout_ref,
                       o_ref, pf_ref, proj_ref, mix_ref, st_ref, xext_ref, hprev_ref, *, layer):
    @pl.when(pl.program_id(1) == 0)
    def _():
        st_ref[...] = jnp.zeros_like(st_ref)
        xext_ref[0:SUBLANES, :] = jnp.zeros((SUBLANES, B_WIDTH), F32)
        hprev_ref[...] = jnp.zeros_like(hprev_ref)

    gin = gin_ref[...]
    n_sub = x_ref.shape[1] // MIX_SUB

    def project(s):
        h = _rms_rows(x_ref[0, s * MIX_SUB:(s + 1) * MIX_SUB, :], gin).astype(BF16)
        pf_ref[s] = _dot(h, wf_ref[...])
        yield
        for blk in range(proj_ref.shape[1]):
            proj_ref[s, blk] = _dot(h, win_ref[:, blk * A_WIDTH:(blk + 1) * A_WIDTH]).astype(BF16)
            yield

    _interleave(project(0))
    for s in range(n_sub):
        f_logit = pf_ref[s]
        q, v, g, x_br, y_br = [proj_ref[s, blk] for blk in range(proj_ref.shape[1])]
        rows = slice(s * MIX_SUB, (s + 1) * MIX_SUB)
        mix_sub = mix_ref.at[rows]
        streams = [
            _hgrn_body(q, f_logit, v, g, lbz_ref, gain_ref, mix_sub, st_ref, layer=layer),
            _rglru_body(x_br, y_br, cw_ref, cb_ref, wg_ref, bg_ref, lam_ref, mix_sub, xext_ref, hprev_ref)]
        if s + 1 < n_sub:
            streams.append(project(s + 1))
        _interleave(*streams)
        mixed = _dot(mix_ref[rows, :], wout_ref[...])
        o_ref[0, rows, :] = x_ref[0, rows, :] + _rms_rows(mixed, gout_ref[...])


def _even_mixer(x3, g_in, w_f, w_main, lb_logits, norm_gain, conv_w, conv_b, w_gates, b_gates,
                lam, w_out, g_out, layer):
    b, t, d = x3.shape
    fixed = lambda bi, ti: (0, 0)
    full = lambda a: pl.BlockSpec(a.shape, fixed)
    resident = lambda a: pl.BlockSpec(a.shape, fixed, pipeline_mode=pl.Buffered(1))
    row_tile = pl.BlockSpec((1, SEQ_TILE, d), lambda bi, ti: (bi, ti, 0))
    return pl.pallas_call(
        functools.partial(_even_mixer_kernel, layer=layer),
        grid=(b, t // SEQ_TILE),
        in_specs=[row_tile, full(g_in), resident(w_f), resident(w_main),
                  full(lb_logits), full(norm_gain), full(conv_w), full(conv_b),
                  resident(w_gates), full(b_gates), full(lam), resident(w_out), full(g_out)],
        out_specs=row_tile,
        out_shape=jax.ShapeDtypeStruct((b, t, d), F32),
        scratch_shapes=[pltpu.VMEM((SEQ_TILE // MIX_SUB, MIX_SUB, A_WIDTH), F32),
                        pltpu.VMEM((SEQ_TILE // MIX_SUB, w_main.shape[1] // A_WIDTH, MIX_SUB, A_WIDTH), BF16),
                        pltpu.VMEM((SEQ_TILE, d), BF16),
                        pltpu.VMEM((A_HEADS, A_DIM, A_DIM), F32),
                        pltpu.VMEM((SEQ_TILE + SUBLANES, B_WIDTH), F32),
                        pltpu.VMEM((1, B_WIDTH), F32)],
        compiler_params=_cparams(("parallel", "arbitrary")),
        name="even_mixer",
    )(x3, g_in, w_f, w_main, lb_logits, norm_gain, conv_w, conv_b, w_gates, b_gates, lam,
      w_out, g_out)


def _block_diag_dense(w):
    nb, bd, _ = w.shape
    eye = jnp.eye(nb, dtype=w.dtype)
    return jnp.einsum('nij,nm->nimj', w, eye).reshape(nb * bd, nb * bd)


def _fox_cumsum_kernel(f_ref, b_ref, o_ref, *, blk):
    t, w = f_ref.shape
    z = f_ref[...] + b_ref[...]
    logsig = jnp.minimum(z, 0.0) - jnp.log(1.0 + jnp.exp(-jnp.abs(z)))
    r = lax.broadcasted_iota(jnp.int32, (blk, blk), 0)
    c = lax.broadcasted_iota(jnp.int32, (blk, blk), 1)
    tri = jnp.where(c <= r, 1.0, 0.0).astype(BF16)
    carry = jnp.zeros((1, w), F32)
    for s in range(0, t, blk):
        x = logsig[s:s + blk]
        hi = x.astype(BF16)
        r1 = x - hi.astype(F32)
        mid = r1.astype(BF16)
        lo = (r1 - mid.astype(F32)).astype(BF16)
        cs = _dot(tri, hi) + _dot(tri, mid) + _dot(tri, lo) + carry
        o_ref[s:s + blk, :] = cs * (-LOG2E)
        carry = cs[blk - 1:blk]


def _fox_cumsum(f_tb, bias_row):
    t, w = f_tb.shape
    return pl.pallas_call(
        functools.partial(_fox_cumsum_kernel, blk=256),
        out_shape=jax.ShapeDtypeStruct((t, w), F32),
        compiler_params=pltpu.CompilerParams(vmem_limit_bytes=VMEM_LIMIT),
        name="fox_cumsum",
    )(f_tb, bias_row)


def _fox_attn_kernel(q_ref, k_ref, vt_ref, c_ref, o_ref, ka_ref, qs_ref, m_ref, acc_ref):
    tq = q_ref.shape[1]
    tk = tq
    n_pairs = q_ref.shape[0] // LANES
    qi = pl.program_id(2)
    dim = lax.broadcasted_iota(jnp.int32, (LANES, tq), 0)
    low = dim < C_HEAD_DIM
    key_row = lax.broadcasted_iota(jnp.int32, (tk, tq), 0)
    qry_col = lax.broadcasted_iota(jnp.int32, (tk, tq), 1)
    diag_ok = key_row <= qry_col

    @pl.when(qi == 0)
    def _():
        rest = c_ref[0, 0]
        pieces = []
        for _ in range(C_PARTS):
            part = rest.astype(BF16)
            pieces.append(part)
            rest = rest - part.astype(F32)
        c_parts = jnp.concatenate(pieces, axis=1)
        src = lax.broadcasted_iota(jnp.int32, (C_PARTS * LANES, LANES), 0)
        dst = lax.broadcasted_iota(jnp.int32, (C_PARTS * LANES, LANES), 1)
        for p in range(n_pairs):
            pick = None
            for hd in range(2):
                for part in range(C_PARTS):
                    hit = (src == part * LANES + 2 * p + hd) & (dst == hd * C_PARTS + part)
                    pick = hit if pick is None else (pick | hit)
            place = jnp.where(pick, 1.0, 0.0).astype(BF16)
            ka_ref[p, :, 0:LANES] = k_ref[0, :, p * LANES:(p + 1) * LANES]
            ka_ref[p, :, LANES:2 * LANES] = _dot(c_parts, place).astype(BF16)

    for p in range(n_pairs):
        q2t = q_ref[p * LANES:(p + 1) * LANES, :]
        zero = jnp.zeros_like(q2t)
        qs_ref[p, 0:LANES, 0:tq] = jnp.where(low, q2t, zero)
        qs_ref[p, 0:LANES, tq:2 * tq] = jnp.where(low, zero, q2t)
        qs_ref[p, LANES:2 * LANES, 0:tq] = jnp.where(dim < C_PARTS, 1.0, 0.0).astype(BF16)
        qs_ref[p, LANES:2 * LANES, tq:2 * tq] = jnp.where(
            (dim >= C_PARTS) & (dim < 2 * C_PARTS), 1.0, 0.0).astype(BF16)
        m_ref[p] = jnp.full((1, 2 * tq), NEG_BIG, F32)
        acc_ref[p] = jnp.zeros((LANES + SUM_ROWS, 2 * tq), F32)

    ones = jnp.ones((SUM_ROWS, tk), BF16)

    def step(j, masked):
        start = pl.multiple_of(j * tk, tk)
        scores = []
        for p in range(n_pairs):
            scores.append(_dot(ka_ref[p, pl.ds(start, tk), :], qs_ref[p]))
        probs, alphas = [], []
        for p in range(n_pairs):
            s = scores[p]
            if masked:
                s = jnp.concatenate([jnp.where(diag_ok, s[:, 0:tq], NEG_BIG),
                                     jnp.where(diag_ok, s[:, tq:2 * tq], NEG_BIG)], axis=1)
            m_prev = m_ref[p]
            m_new = jnp.maximum(m_prev, jnp.max(s, axis=0, keepdims=True))
            alpha = jnp.exp2(m_prev - m_new)
            pr = jnp.exp2(s - m_new)
            m_ref[p] = m_new
            probs.append(pr.astype(BF16))
            alphas.append(alpha)
        for p in range(n_pairs):
            vtb = jnp.concatenate([vt_ref[p * LANES:(p + 1) * LANES, pl.ds(start, tk)], ones], axis=0)
            acc_ref[p] = alphas[p] * acc_ref[p] + _dot(vtb, probs[p])

    def body(j, carry):
        step(j, False)
        return carry

    lax.fori_loop(0, qi, body, 0)
    step(qi, True)

    for p in range(n_pairs):
        o = acc_ref[p, 0:LANES, :] / acc_ref[p, LANES:LANES + 1, :]
        o = jnp.where(low, o[:, 0:tq], o[:, tq:2 * tq])
        o_ref[0, :, p * LANES:(p + 1) * LANES] = o.T.astype(o_ref.dtype)


def _fox_attn(qt, k3, vt, c4):
    b, t, d = k3.shape
    n_q = t // ATT_TILE
    width = ATT_PAIRS * LANES
    row_state = pltpu.VMEM((ATT_PAIRS, 1, 2 * ATT_TILE), F32)
    return pl.pallas_call(
        _fox_attn_kernel,
        grid=(b, d // width, t // ATT_TILE),
        in_specs=[
            pl.BlockSpec((width, ATT_TILE), lambda bi, g, qi: (g, bi * n_q + qi)),
            pl.BlockSpec((1, t, width), lambda bi, g, qi: (bi, 0, g)),
            pl.BlockSpec((width, t), lambda bi, g, qi: (g, bi)),
            pl.BlockSpec((1, 1, t, LANES), lambda bi, g, qi: (bi, g, 0, 0)),
        ],
        out_specs=pl.BlockSpec((1, ATT_TILE, width), lambda bi, g, qi: (bi, qi, g)),
        out_shape=jax.ShapeDtypeStruct((b, t, d), BF16),
        scratch_shapes=[pltpu.VMEM((ATT_PAIRS, t, 2 * LANES), BF16),
                        pltpu.VMEM((ATT_PAIRS, 2 * LANES, 2 * ATT_TILE), BF16), row_state,
                        pltpu.VMEM((ATT_PAIRS, LANES + SUM_ROWS, 2 * ATT_TILE), F32)],
        compiler_params=_cparams(("parallel", "parallel", "arbitrary")),
        name="fox_attn",
    )(qt, k3, vt, c4)


def _ffn_kernel(x_ref, g_in_ref, g_out_ref, wup_ref, cw_ref, wdn_ref, o_ref,
                h_ref, ext_ref, prev_ref, act_ref, acc_ref):
    tm = x_ref.shape[1]
    halo = SUBLANES
    fc = FF_CHUNK
    d_ff = wdn_ref.shape[0]
    n_chunks = d_ff // fc

    @pl.when(pl.program_id(1) == 0)
    def _():
        prev_ref[...] = jnp.zeros_like(prev_ref)

    x = x_ref[0]
    h_ref[...] = _rms_rows(x, g_in_ref[...]).astype(BF16)
    acc_ref[...] = jnp.zeros_like(acc_ref)

    def issue_up(j):
        ext = ext_ref.at[j % 2]
        ext[0:halo, :] = prev_ref[j]
        for half in range(2):
            w0 = half * d_ff + j * fc
            ext[halo:halo + tm, half * fc:(half + 1) * fc] = _dot(h_ref[...], wup_ref[:, w0:w0 + fc])

    def issue_down(j):
        acc_ref[...] += _dot(act_ref[j % 2], wdn_ref[j * fc:(j + 1) * fc, :])

    issue_up(0)
    for j in range(n_chunks):
        if j + 1 < n_chunks:
            issue_up(j + 1)
        if j >= 1:
            issue_down(j - 1)
        ext = ext_ref.at[j % 2]
        prev_ref[j] = ext[tm:tm + halo, :]
        cw = jnp.concatenate([cw_ref[:, j * fc:(j + 1) * fc],
                              cw_ref[:, d_ff + j * fc:d_ff + (j + 1) * fc]], axis=1)
        for r0 in range(0, tm, FFN_ROWS):
            base = halo + r0
            conv = cw[FFN_CONV:FFN_CONV + 1] + cw[FFN_CONV - 1:FFN_CONV] * ext[base:base + FFN_ROWS, :]
            for s in range(1, FFN_CONV):
                conv = conv + cw[FFN_CONV - 1 - s:FFN_CONV - s] * ext[base - s:base - s + FFN_ROWS, :]
            act_ref[j % 2, r0:r0 + FFN_ROWS, :] = (_gelu_tanh_x2(conv[:, :fc]) * conv[:, fc:]).astype(BF16)
    issue_down(n_chunks - 1)
    o_ref[0] = x + _rms_rows(acc_ref[...], g_out_ref[...])


def _ffn(x3, g_in, g_out, w_up, taps, w_down):
    b, t, d = x3.shape
    n_chunks = w_down.shape[0] // FF_CHUNK
    fc2 = 2 * FF_CHUNK
    fixed2 = lambda bi, ti: (0, 0)
    return pl.pallas_call(
        _ffn_kernel,
        grid=(b, t // FFN_TILE),
        in_specs=[
            pl.BlockSpec((1, FFN_TILE, d), lambda bi, ti: (bi, ti, 0)),
            pl.BlockSpec((1, d), fixed2),
            pl.BlockSpec((1, d), fixed2),
            pl.BlockSpec(w_up.shape, fixed2, pipeline_mode=pl.Buffered(1)),
            pl.BlockSpec(taps.shape, fixed2, pipeline_mode=pl.Buffered(1)),
            pl.BlockSpec(w_down.shape, fixed2, pipeline_mode=pl.Buffered(1)),
        ],
        out_specs=pl.BlockSpec((1, FFN_TILE, d), lambda bi, ti: (bi, ti, 0)),
        out_shape=jax.ShapeDtypeStruct((b, t, d), F32),
        scratch_shapes=[
            pltpu.VMEM((FFN_TILE, d), BF16),
            pltpu.VMEM((2, FFN_TILE + SUBLANES, fc2), F32),
            pltpu.VMEM((n_chunks, SUBLANES, fc2), F32),
            pltpu.VMEM((2, FFN_TILE, fc2 // 2), BF16),
            pltpu.VMEM((FFN_TILE, d), F32),
        ],
        compiler_params=_cparams(("parallel", "arbitrary")),
        name="conv_ffn",
    )(x3, g_in, g_out, w_up, taps, w_down)


def _ffn_taps(conv_w, conv_b):
    taps = jnp.concatenate([conv_w, conv_b[None, :]], axis=0)
    taps = jnp.concatenate([taps[:, :D_FF], 0.5 * taps[:, D_FF:]], axis=1)
    return jnp.pad(taps, ((0, SUBLANES - taps.shape[0]), (0, 0)))


def kernel(x, norm_gains, even_w_in, hgrn_lb_logits, hgrn_norm, rg_conv_w, rg_conv_b, rg_wa, rg_ba,
           rg_wx, rg_bx, rg_lambda, even_w_out, odd_w_in, fox_f_bias, odd_w_out,
           ffn_w_up, ffn_conv_w, ffn_conv_b, ffn_w_down):
    b, t, d = x.shape
    n = b * t
    x2 = x.reshape(n, d)
    for l in range(DEPTH):
        g = norm_gains[l]
        if l % 2 == 0:
            e = l // 2
            w_in = even_w_in[e]
            a2 = 2 * A_WIDTH
            w_main = jnp.concatenate([w_in[:, :A_WIDTH], w_in[:, a2:]], axis=1).astype(BF16)
            w_gates = jnp.concatenate(
                [_block_diag_dense(rg_wa[e]), _block_diag_dense(rg_wx[e])], axis=1).astype(BF16)
            b_gates = jnp.concatenate([rg_ba[e], rg_bx[e]])[None, :]
            x2 = _even_mixer(x2.reshape(b, t, d), g[0:1], w_in[:, A_WIDTH:a2].astype(BF16), w_main,
                             hgrn_lb_logits, hgrn_norm[e][None, :],
                             rg_conv_w[e], rg_conv_b[e][None, :], w_gates, b_gates,
                             rg_lambda[e][None, :], even_w_out[e].astype(BF16), g[1:2], l).reshape(n, d)
        else:
            o = l // 2
            w_in = odd_w_in[o]
            wqt = w_in[:, :d].T.astype(BF16)
            wk = w_in[:, d:2 * d].astype(BF16)
            wvt = w_in[:, 2 * d:3 * d].T.astype(BF16)
            wf = jnp.pad(w_in[:, 3 * d:], ((0, 0), (0, LANES - C_HEADS))).astype(BF16)
            qt, k, vt, f = _odd_inproj(x2, g[0:1], wqt, wk, wvt, wf)
            f_tb = f.reshape(b, t, LANES)[:, :, :C_HEADS].transpose(1, 0, 2).reshape(t, b * C_HEADS)
            bias_row = jnp.tile(fox_f_bias[o], b)[None, :]
            c_tb = _fox_cumsum(f_tb, bias_row)
            group = 2 * ATT_PAIRS
            c4 = c_tb.reshape(t, b, C_HEADS // group, group).transpose(1, 2, 0, 3)
            c4 = jnp.pad(c4, ((0, 0), (0, 0), (0, 0), (0, LANES - group)))
            att = _fox_attn(qt, k.reshape(b, t, d), vt, c4)
            x2 = _outproj([att.reshape(n, d)], [odd_w_out[o].astype(BF16)], x2, g[1:2])
        x2 = _ffn(x2.reshape(b, t, d), g[2:3], g[3:4], ffn_w_up[l].astype(BF16),
                  _ffn_taps(ffn_conv_w[l], ffn_conv_b[l]), ffn_w_down[l].astype(BF16)).reshape(n, d)
    return x2.reshape(b, t, d)
```

```python
import functools

import jax
import jax.numpy as jnp
from jax import lax
from jax.experimental import pallas as pl
from jax.experimental.pallas import tpu as pltpu

F32 = jnp.float32
BF16 = jnp.bfloat16

D_MODEL = 1024
DEPTH = 2
A_HEADS = 4
A_DIM = 128
A_WIDTH = A_HEADS * A_DIM
HGRN_CHUNK = 64
B_WIDTH = D_MODEL - A_WIDTH
B_BLOCKS = 8
B_BLOCK_DIM = B_WIDTH // B_BLOCKS
B_CONV = 4
RG_C = 8.0
C_HEADS = 16
C_HEAD_DIM = D_MODEL // C_HEADS
D_FF = 2816
FFN_CONV = 3
EPS = 1e-6

LANES = 128
SUBLANES = 8
VMEM_LIMIT = 56 * 1024 * 1024

ROW_TILE = 1024
SEQ_TILE = 512
MIX_SUB = 256
RG_ROWS = 64
FFN_SUBS = 1
FFN_TILE = 512
FF_CHUNK = 256
FFN_PLANES = 8
ATT_TILE = 256
ATT_PAIRS = 8
NEG_BIG = -1e30

LOG2E = 1.4426950408889634
C_PARTS = 3
SUM_ROWS = 16
GELU_C0 = 0.7978845608028654
GELU_C1 = 0.044715


def _cparams(semantics):
    return pltpu.CompilerParams(dimension_semantics=semantics, vmem_limit_bytes=VMEM_LIMIT)


def _rms_rows(x, gain):
    ms = jnp.mean(x * x, axis=-1, keepdims=True)
    return x * lax.rsqrt(ms + EPS) * gain


def _gelu_tanh_x2(x):
    inner = x * (GELU_C0 + (GELU_C0 * GELU_C1) * (x * x))
    return x * (1.0 + jnp.tanh(inner))


def _gelu_tanh(x):
    return 0.5 * _gelu_tanh_x2(x)


def _sigmoid(x):
    return 1.0 / (1.0 + jnp.exp(-x))


def _dot(a, b):
    return jnp.dot(a, b, preferred_element_type=F32)


def _dot_nt(a, b):
    return lax.dot_general(a, b, (((1,), (1,)), ((), ())), preferred_element_type=F32)


def _dot_tn(a, b):
    return lax.dot_general(a, b, (((0,), (0,)), ((), ())), preferred_element_type=F32)


def _interleave(*streams):
    live = list(streams)
    while live:
        for stream in list(live):
            if next(stream, _DONE) is _DONE:
                live.remove(stream)


_DONE = object()


def _split_bf16(x):
    hi = x.astype(BF16)
    lo = (x - hi.astype(F32)).astype(BF16)
    return hi, lo


def _odd_inproj_kernel(x_ref, g_ref, wqt_ref, wk_ref, wvt_ref, wf_ref, qt_ref, k_ref, vt_ref, f_ref):
    h = _rms_rows(x_ref[...], g_ref[...]).astype(BF16)
    qt_ref[...] = (_dot_nt(wqt_ref[...], h) * (LOG2E * C_HEAD_DIM ** -0.5)).astype(BF16)
    k_ref[...] = _dot(h, wk_ref[...]).astype(BF16)
    vt_ref[...] = _dot_nt(wvt_ref[...], h).astype(BF16)
    f_ref[...] = _dot(h, wf_ref[...])


def _odd_inproj(x2, gain, wqt_bf16, wk_bf16, wvt_bf16, wf_bf16):
    n, d = x2.shape
    row = lambda i: (i, 0)
    col = lambda i: (0, i)
    fixed = lambda i: (0, 0)
    return pl.pallas_call(
        _odd_inproj_kernel,
        grid=(n // ROW_TILE,),
        in_specs=[
            pl.BlockSpec((ROW_TILE, d), row),
            pl.BlockSpec((1, d), fixed),
            pl.BlockSpec((d, d), fixed),
            pl.BlockSpec((d, d), fixed),
            pl.BlockSpec((d, d), fixed),
            pl.BlockSpec((d, LANES), fixed),
        ],
        out_specs=[
            pl.BlockSpec((d, ROW_TILE), col),
            pl.BlockSpec((ROW_TILE, d), row),
            pl.BlockSpec((d, ROW_TILE), col),
            pl.BlockSpec((ROW_TILE, LANES), row),
        ],
        out_shape=[
            jax.ShapeDtypeStruct((d, n), BF16),
            jax.ShapeDtypeStruct((n, d), BF16),
            jax.ShapeDtypeStruct((d, n), BF16),
            jax.ShapeDtypeStruct((n, LANES), F32),
        ],
        compiler_params=_cparams(("parallel",)),
        name="odd_inproj",
    )(x2, gain, wqt_bf16, wk_bf16, wvt_bf16, wf_bf16)


def _outproj_kernel(*refs, n_lhs):
    a_refs = refs[:n_lhs]
    w_refs = refs[n_lhs:2 * n_lhs]
    x_ref, g_ref, o_ref = refs[2 * n_lhs:]
    mix = _dot(a_refs[0][...].astype(BF16), w_refs[0][...])
    for a_ref, w_ref in zip(a_refs[1:], w_refs[1:]):
        mix = mix + _dot(a_ref[...].astype(BF16), w_ref[...])
    o_ref[...] = x_ref[...] + _rms_rows(mix, g_ref[...])


def _outproj(lhs_list, w_list, x2, gain):
    n, d = x2.shape
    n_lhs = len(lhs_list)
    row = lambda i: (i, 0)
    fixed = lambda i: (0, 0)
    in_specs = [pl.BlockSpec((ROW_TILE, a.shape[1]), row) for a in lhs_list]
    in_specs += [pl.BlockSpec(w.shape, fixed) for w in w_list]
    in_specs += [pl.BlockSpec((ROW_TILE, d), row), pl.BlockSpec((1, d), fixed)]
    return pl.pallas_call(
        functools.partial(_outproj_kernel, n_lhs=n_lhs),
        grid=(n // ROW_TILE,),
        in_specs=in_specs,
        out_specs=pl.BlockSpec((ROW_TILE, d), row),
        out_shape=jax.ShapeDtypeStruct((n, d), F32),
        compiler_params=_cparams(("parallel",)),
        name="outproj",
    )(*lhs_list, *w_list, x2, gain)


def _hgrn_body(q, f_logit, v, g, lbz_ref, gain_ref, mix_ref, st_ref, *, layer):
    tt = q.shape[0]

    z = lbz_ref[...]
    ez = jnp.exp(z - jnp.max(z, axis=0, keepdims=True))
    lb = jnp.sum(ez[:layer + 1], axis=0, keepdims=True) / jnp.sum(ez, axis=0, keepdims=True)

    forget = lb + (1.0 - lb) * _sigmoid(f_logit)
    logf = jnp.log(forget)

    r = lax.broadcasted_iota(jnp.int32, (tt, tt), 0)
    c = lax.broadcasted_iota(jnp.int32, (tt, tt), 1)
    shift = HGRN_CHUNK.bit_length() - 1
    same_chunk = lax.shift_right_logical(r, shift) == lax.shift_right_logical(c, shift)
    tri = jnp.where((c <= r) & same_chunk, 1.0, 0.0).astype(BF16)
    hi, lo = _split_bf16(logf)
    bcum = _dot(tri, hi) + _dot(tri, lo)

    qs = q.astype(F32)
    qs = qs * _sigmoid(qs)
    kk = 1.0 - forget
    vv = v
    gate = _sigmoid(g.astype(F32))
    gain = gain_ref[...]

    cr = lax.broadcasted_iota(jnp.int32, (HGRN_CHUNK, HGRN_CHUNK), 0)
    cc = lax.broadcasted_iota(jnp.int32, (HGRN_CHUNK, HGRN_CHUNK), 1)
    causal = cc <= cr
    yield

    for ch in range(tt // HGRN_CHUNK):
        rows = slice(ch * HGRN_CHUNK, (ch + 1) * HGRN_CHUNK)
        bc = bcum[rows]
        b_last = bc[HGRN_CHUNK - 1:HGRN_CHUNK]
        q_dec = (qs[rows] * jnp.exp(bc)).astype(BF16)
        k_intra = (kk[rows] * jnp.exp(-bc)).astype(BF16)
        k_upd = (kk[rows] * jnp.exp(b_last - bc)).astype(BF16)
        dec = jnp.exp(b_last)
        v_bf = vv[rows]
        outs = []
        for hd in range(A_HEADS):
            cols = slice(hd * A_DIM, (hd + 1) * A_DIM)
            st = st_ref[hd]
            scores = jnp.where(causal, _dot_nt(q_dec[:, cols], k_intra[:, cols]), 0.0)
            o = _dot(scores.astype(BF16), v_bf[:, cols]) + _dot_nt(q_dec[:, cols], st.astype(BF16))
            st_ref[hd] = dec[:, cols] * st + _dot_tn(v_bf[:, cols], k_upd[:, cols])
            o = o * lax.rsqrt(jnp.mean(o * o, axis=-1, keepdims=True) + EPS)
            outs.append(o)
        o_all = jnp.concatenate(outs, axis=-1)
        mix_ref[rows, 0:A_WIDTH] = (o_all * gain * gate[rows]).astype(mix_ref.dtype)
        yield


def _rglru_body(x_br, y_br, cw_ref, cb_ref, wg_ref, bg_ref, lam_ref, mix_ref,
                xext_ref, hprev_ref):
    tt = x_br.shape[0]
    halo = SUBLANES
    x = x_br.astype(F32)
    xext_ref[halo:halo + tt, :] = x
    cw = cw_ref[...]
    xf = cb_ref[...] + cw[B_CONV - 1:B_CONV] * x
    for j in range(1, B_CONV):
        xf = xf + cw[B_CONV - 1 - j:B_CONV - j] * xext_ref[halo - j:halo - j + tt, :]
    xext_ref[0:halo, :] = x[tt - halo:tt]

    gates = _dot(xf.astype(BF16), wg_ref[...]) + bg_ref[...]
    lam = lam_ref[...]
    softplus_neg = jnp.maximum(-lam, 0.0) + jnp.log(1.0 + jnp.exp(-jnp.abs(lam)))
    yield

    groups = RG_ROWS // SUBLANES
    row = lax.broadcasted_iota(jnp.int32, (groups, SUBLANES, B_WIDTH), 1)
    carry = hprev_ref[...]
    for r0 in range(0, tt, RG_ROWS):
        rows = slice(r0, r0 + RG_ROWS)
        r = _sigmoid(gates[rows, :B_WIDTH])
        i = _sigmoid(gates[rows, B_WIDTH:])
        log_a = (-RG_C) * r * softplus_neg
        a = jnp.exp(log_a)
        u = jnp.sqrt(1.0 - jnp.exp(2.0 * log_a)) * (i * xf[rows])
        a3 = a.reshape(groups, SUBLANES, B_WIDTH)
        u3 = u.reshape(groups, SUBLANES, B_WIDTH)
        d = 1
        while d < SUBLANES:
            keep = row >= d
            a_sh = jnp.where(keep, pltpu.roll(a3, d, 1), 1.0)
            u_sh = jnp.where(keep, pltpu.roll(u3, d, 1), 0.0)
            u3 = u3 + a3 * u_sh
            a3 = a3 * a_sh
            d *= 2
        hs = []
        for gi in range(groups):
            hg = a3[gi] * carry + u3[gi]
            hs.append(hg)
            carry = hg[SUBLANES - 1:SUBLANES]
        h = jnp.concatenate(hs, axis=0)
        mix_ref[rows, A_WIDTH:D_MODEL] = (h * _gelu_tanh(y_br[rows].astype(F32))).astype(mix_ref.dtype)
        yield
    hprev_ref[...] = carry


def _even_mixer_kernel(x_ref, gin_ref, wf_ref, win_ref, lbz_ref, gain_ref,
                       cw_ref, cb_ref, wg_ref, bg_ref, lam_ref, wout_ref, gout_ref,
                       o_ref, pf_ref, proj_ref, mix_ref, st_ref, xext_ref, hprev_ref, *, layer):
    @pl.when(pl.program_id(1) == 0)
    def _():
        st_ref[...] = jnp.zeros_like(st_ref)
        xext_ref[0:SUBLANES, :] = jnp.zeros((SUBLANES, B_WIDTH), F32)
        hprev_ref[...] = jnp.zeros_like(hprev_ref)

    gin = gin_ref[...]
    n_sub = x_ref.shape[1] // MIX_SUB

    def project(s):
        h = _rms_rows(x_ref[0, s * MIX_SUB:(s + 1) * MIX_SUB, :], gin).astype(BF16)
        pf_ref[s] = _dot(h, wf_ref[...])
        yield
        for blk in range(proj_ref.shape[1]):
            proj_ref[s, blk] = _dot(h, win_ref[:, blk * A_WIDTH:(blk + 1) * A_WIDTH]).astype(BF16)
            yield

    _interleave(project(0))
    for s in range(n_sub):
        f_logit = pf_ref[s]
        q, v, g, x_br, y_br = [proj_ref[s, blk] for blk in range(proj_ref.shape[1])]
        rows = slice(s * MIX_SUB, (s + 1) * MIX_SUB)
        mix_sub = mix_ref.at[rows]
        streams = [
            _hgrn_body(q, f_logit, v, g, lbz_ref, gain_ref, mix_sub, st_ref, layer=layer),
            _rglru_body(x_br, y_br, cw_ref, cb_ref, wg_ref, bg_ref, lam_ref, mix_sub, xext_ref, hprev_ref)]
        if s + 1 < n_sub:
            streams.append(project(s + 1))
        _interleave(*streams)
        mixed = _dot(mix_ref[rows, :], wout_ref[...])
        o_ref[0, rows, :] = x_ref[0, rows, :] + _rms_rows(mixed, gout_ref[...])


def _even_mixer(x3, g_in, w_f, w_main, lb_logits, norm_gain, conv_w, conv_b, w_gates, b_gates,
                lam, w_out, g_out, layer):
    b, t, d = x3.shape
    fixed = lambda bi, ti: (0, 0)
    full = lambda a: pl.BlockSpec(a.shape, fixed)
    resident = lambda a: pl.BlockSpec(a.shape, fixed, pipeline_mode=pl.Buffered(1))
    row_tile = pl.BlockSpec((1, SEQ_TILE, d), lambda bi, ti: (bi, ti, 0))
    return pl.pallas_call(
        functools.partial(_even_mixer_kernel, layer=layer),
        grid=(b, t // SEQ_TILE),
        in_specs=[row_tile, full(g_in), resident(w_f), resident(w_main),
                  full(lb_logits), full(norm_gain), full(conv_w), full(conv_b),
                  resident(w_gates), full(b_gates), full(lam), resident(w_out), full(g_out)],
        out_specs=row_tile,
        out_shape=jax.ShapeDtypeStruct((b, t, d), F32),
        scratch_shapes=[pltpu.VMEM((SEQ_TILE // MIX_SUB, MIX_SUB, A_WIDTH), F32),
                        pltpu.VMEM((SEQ_TILE // MIX_SUB, w_main.shape[1] // A_WIDTH, MIX_SUB, A_WIDTH), BF16),
                        pltpu.VMEM((SEQ_TILE, d), BF16),
                        pltpu.VMEM((A_HEADS, A_DIM, A_DIM), F32),
                        pltpu.VMEM((SEQ_TILE + SUBLANES, B_WIDTH), F32),
                        pltpu.VMEM((1, B_WIDTH), F32)],
        compiler_params=_cparams(("parallel", "arbitrary")),
        name="even_mixer",
    )(x3, g_in, w_f, w_main, lb_logits, norm_gain, conv_w, conv_b, w_gates, b_gates, lam,
      w_out, g_out)


def _block_diag_dense(w):
    nb, bd, _ = w.shape
    eye = jnp.eye(nb, dtype=w.dtype)
    return jnp.einsum('nij,nm->nimj', w, eye).reshape(nb * bd, nb * bd)


def _fox_cumsum_kernel(f_ref, b_ref, o_ref, *, blk):
    t, w = f_ref.shape
    z = f_ref[...] + b_ref[...]
    logsig = jnp.minimum(z, 0.0) - jnp.log(1.0 + jnp.exp(-jnp.abs(z)))
    r = lax.broadcasted_iota(jnp.int32, (blk, blk), 0)
    c = lax.broadcasted_iota(jnp.int32, (blk, blk), 1)
    tri = jnp.where(c <= r, 1.0, 0.0).astype(BF16)
    carry = jnp.zeros((1, w), F32)
    for s in range(0, t, blk):
        x = logsig[s:s + blk]
        hi = x.astype(BF16)
        r1 = x - hi.astype(F32)
        mid = r1.astype(BF16)
        lo = (r1 - mid.astype(F32)).astype(BF16)
        cs = _dot(tri, hi) + _dot(tri, mid) + _dot(tri, lo) + carry
        o_ref[s:s + blk, :] = cs * (-LOG2E)
        carry = cs[blk - 1:blk]


def _fox_cumsum(f_tb, bias_row):
    t, w = f_tb.shape
    return pl.pallas_call(
        functools.partial(_fox_cumsum_kernel, blk=256),
        out_shape=jax.ShapeDtypeStruct((t, w), F32),
        compiler_params=pltpu.CompilerParams(vmem_limit_bytes=VMEM_LIMIT),
        name="fox_cumsum",
    )(f_tb, bias_row)


def _fox_attn_kernel(q_ref, k_ref, vt_ref, c_ref, o_ref, ka_ref, qs_ref, m_ref, acc_ref):
    tq = q_ref.shape[1]
    tk = tq
    n_pairs = q_ref.shape[0] // LANES
    qi = pl.program_id(2)
    dim = lax.broadcasted_iota(jnp.int32, (LANES, tq), 0)
    low = dim < C_HEAD_DIM
    key_row = lax.broadcasted_iota(jnp.int32, (tk, tq), 0)
    qry_col = lax.broadcasted_iota(jnp.int32, (tk, tq), 1)
    diag_ok = key_row <= qry_col

    @pl.when(qi == 0)
    def _():
        rest = c_ref[0, 0]
        n_heads = rest.shape[1]
        pieces = []
        for _ in range(C_PARTS):
            part = rest.astype(BF16)
            pieces.append(part)
            rest = rest - part.astype(F32)
        c_parts = jnp.concatenate(pieces, axis=1)
        src = lax.broadcasted_iota(jnp.int32, (C_PARTS * n_heads, LANES), 0)
        dst = lax.broadcasted_iota(jnp.int32, (C_PARTS * n_heads, LANES), 1)
        for p in range(n_pairs):
            pick = None
            for hd in range(2):
                for part in range(C_PARTS):
                    hit = (src == part * n_heads + 2 * p + hd) & (dst == hd * C_PARTS + part)
                    pick = hit if pick is None else (pick | hit)
            place = jnp.where(pick, 1.0, 0.0).astype(BF16)
            ka_ref[p, :, 0:LANES] = k_ref[0, :, p * LANES:(p + 1) * LANES]
            ka_ref[p, :, LANES:2 * LANES] = _dot(c_parts, place).astype(BF16)

    for p in range(n_pairs):
        q2t = q_ref[p * LANES:(p + 1) * LANES, :]
        zero = jnp.zeros_like(q2t)
        qs_ref[p, 0:LANES, 0:tq] = jnp.where(low, q2t, zero)
        qs_ref[p, 0:LANES, tq:2 * tq] = jnp.where(low, zero, q2t)
        qs_ref[p, LANES:2 * LANES, 0:tq] = jnp.where(dim < C_PARTS, 1.0, 0.0).astype(BF16)
        qs_ref[p, LANES:2 * LANES, tq:2 * tq] = jnp.where(
            (dim >= C_PARTS) & (dim < 2 * C_PARTS), 1.0, 0.0).astype(BF16)
        m_ref[p] = jnp.full((1, 2 * tq), NEG_BIG, F32)
        acc_ref[p] = jnp.zeros((LANES + SUM_ROWS, 2 * tq), F32)

    ones = jnp.ones((SUM_ROWS, tk), BF16)

    def step(j, masked):
        start = pl.multiple_of(j * tk, tk)
        scores = []
        for p in range(n_pairs):
            scores.append(_dot(ka_ref[p, pl.ds(start, tk), :], qs_ref[p]))
        probs, alphas = [], []
        for p in range(n_pairs):
            s = scores[p]
            if masked:
                s = jnp.concatenate([jnp.where(diag_ok, s[:, 0:tq], NEG_BIG),
                                     jnp.where(diag_ok, s[:, tq:2 * tq], NEG_BIG)], axis=1)
            m_prev = m_ref[p]
            m_new = jnp.maximum(m_prev, jnp.max(s, axis=0, keepdims=True))
            alpha = jnp.exp2(m_prev - m_new)
            pr = jnp.exp2(s - m_new)
            m_ref[p] = m_new
            probs.append(pr.astype(BF16))
            alphas.append(alpha)
        for p in range(n_pairs):
            vtb = jnp.concatenate([vt_ref[p * LANES:(p + 1) * LANES, pl.ds(start, tk)], ones], axis=0)
            acc_ref[p] = alphas[p] * acc_ref[p] + _dot(vtb, probs[p])

    def body(j, carry):
        step(j, False)
        return carry

    lax.fori_loop(0, qi, body, 0)
    step(qi, True)

    for p in range(n_pairs):
        o = acc_ref[p, 0:LANES, :] / acc_ref[p, LANES:LANES + 1, :]
        o = jnp.where(low, o[:, 0:tq], o[:, tq:2 * tq])
        o_ref[0, :, p * LANES:(p + 1) * LANES] = o.T.astype(o_ref.dtype)


def _fox_attn(qt, k3, vt, c4):
    b, t, d = k3.shape
    n_q = t // ATT_TILE
    width = ATT_PAIRS * LANES
    row_state = pltpu.VMEM((ATT_PAIRS, 1, 2 * ATT_TILE), F32)
    return pl.pallas_call(
        _fox_attn_kernel,
        grid=(b, d // width, t // ATT_TILE),
        in_specs=[
            pl.BlockSpec((width, ATT_TILE), lambda bi, g, qi: (g, bi * n_q + qi)),
            pl.BlockSpec((1, t, width), lambda bi, g, qi: (bi, 0, g)),
            pl.BlockSpec((width, t), lambda bi, g, qi: (g, bi)),
            pl.BlockSpec((1, 1, t, 2 * ATT_PAIRS), lambda bi, g, qi: (bi, g, 0, 0)),
        ],
        out_specs=pl.BlockSpec((1, ATT_TILE, width), lambda bi, g, qi: (bi, qi, g)),
        out_shape=jax.ShapeDtypeStruct((b, t, d), BF16),
        scratch_shapes=[pltpu.VMEM((ATT_PAIRS, t, 2 * LANES), BF16),
                        pltpu.VMEM((ATT_PAIRS, 2 * LANES, 2 * ATT_TILE), BF16), row_state,
                        pltpu.VMEM((ATT_PAIRS, LANES + SUM_ROWS, 2 * ATT_TILE), F32)],
        compiler_params=_cparams(("parallel", "parallel", "arbitrary")),
        name="fox_attn",
    )(qt, k3, vt, c4)


def _ffn_kernel(x_ref, g_in_ref, g_out_ref, wup_ref, cw_ref, wdn_ref, o_ref,
                xs_ref, h_ref, ext_ref, prev_ref, act_ref, acc_ref):
    @pl.when(pl.program_id(1) == 0)
    def _():
        prev_ref[...] = jnp.zeros_like(prev_ref)

    streams = [_ffn_stream(sub, x_ref, g_in_ref, g_out_ref, wup_ref, cw_ref, wdn_ref, o_ref,
                           xs_ref.at[sub], h_ref.at[sub], ext_ref.at[sub], prev_ref,
                           act_ref.at[sub], acc_ref.at[sub])
               for sub in range(xs_ref.shape[0])]
    _interleave(*streams)


def _ffn_stream(sub, x_ref, g_in_ref, g_out_ref, wup_ref, cw_ref, wdn_ref, o_ref,
                xs_ref, h_ref, ext_ref, prev_ref, act_ref, acc_ref):
    tm, d = h_ref.shape
    row0 = sub * tm
    halo = SUBLANES
    fc = FF_CHUNK
    d_ff = wdn_ref.shape[0]
    n_chunks = d_ff // fc
    n_slabs = d // LANES
    planes = FFN_PLANES
    ra = tm // planes
    halo_planes = tuple(range(planes - (FFN_CONV - 1), planes))

    for _ in range(sub):
        yield

    for k in range(n_slabs):
        xs_ref[k] = x_ref[0, row0:row0 + tm, k * LANES:(k + 1) * LANES]

    def x_planes():
        return jnp.concatenate(
            [jnp.concatenate([xs_ref[k, pl.ds(b, ra, stride=planes), :] for k in range(n_slabs)], axis=1)
             for b in range(planes)], axis=0)

    h_ref[...] = _rms_rows(x_planes(), g_in_ref[...]).astype(BF16)
    acc_ref[...] = jnp.zeros_like(acc_ref)

    def issue_up(j):
        ext = ext_ref.at[j % 2]
        for i, b in enumerate(halo_planes):
            ext[b, 0:halo, :] = prev_ref[j, i]
        for half in range(2):
            w0 = half * d_ff + j * fc
            up = _dot(h_ref[...], wup_ref[:, w0:w0 + fc])
            for b in range(planes):
                ext[b, halo:halo + ra, half * fc:(half + 1) * fc] = up[b * ra:(b + 1) * ra]

    def issue_down(j):
        acc_ref[...] += _dot(act_ref[j % 2], wdn_ref[j * fc:(j + 1) * fc, :])

    issue_up(0)
    yield
    for j in range(n_chunks):
        if j + 1 < n_chunks:
            issue_up(j + 1)
        if j >= 1:
            issue_down(j - 1)
        ext = ext_ref.at[j % 2]
        for i, b in enumerate(halo_planes):
            prev_ref[j, i] = ext[b, ra:ra + halo, :]
        cw = jnp.concatenate([cw_ref[:, j * fc:(j + 1) * fc],
                              cw_ref[:, d_ff + j * fc:d_ff + (j + 1) * fc]], axis=1)
        for b in range(planes):
            conv = cw[FFN_CONV:FFN_CONV + 1] + cw[FFN_CONV - 1:FFN_CONV] * ext[b, halo:halo + ra, :]
            for s in range(1, FFN_CONV):
                if b >= s:
                    shifted = ext[b - s, halo:halo + ra, :]
                else:
                    shifted = ext[b - s + planes, halo - 1:halo - 1 + ra, :]
                conv = conv + cw[FFN_CONV - 1 - s:FFN_CONV - s] * shifted
            act_ref[j % 2, b * ra:(b + 1) * ra, :] = (_gelu_tanh_x2(conv[:, :fc]) * conv[:, fc:]).astype(BF16)
        yield
    issue_down(n_chunks - 1)

    out = x_planes() + _rms_rows(acc_ref[...], g_out_ref[...])
    for b in range(planes):
        for k in range(n_slabs):
            xs_ref[k, pl.ds(b, ra, stride=planes), :] = out[b * ra:(b + 1) * ra, k * LANES:(k + 1) * LANES]
    for k in range(n_slabs):
        o_ref[0, row0:row0 + tm, k * LANES:(k + 1) * LANES] = xs_ref[k]


def _ffn(x3, g_in, g_out, w_up, taps, w_down):
    b, t, d = x3.shape
    n_chunks = w_down.shape[0] // FF_CHUNK
    fc2 = 2 * FF_CHUNK
    sub_rows = FFN_TILE // FFN_SUBS
    fixed2 = lambda bi, ti: (0, 0)
    return pl.pallas_call(
        _ffn_kernel,
        grid=(b, t // FFN_TILE),
        in_specs=[
            pl.BlockSpec((1, FFN_TILE, d), lambda bi, ti: (bi, ti, 0)),
            pl.BlockSpec((1, d), fixed2),
            pl.BlockSpec((1, d), fixed2),
            pl.BlockSpec(w_up.shape, fixed2, pipeline_mode=pl.Buffered(1)),
            pl.BlockSpec(taps.shape, fixed2, pipeline_mode=pl.Buffered(1)),
            pl.BlockSpec(w_down.shape, fixed2, pipeline_mode=pl.Buffered(1)),
        ],
        out_specs=pl.BlockSpec((1, FFN_TILE, d), lambda bi, ti: (bi, ti, 0)),
        out_shape=jax.ShapeDtypeStruct((b, t, d), F32),
        scratch_shapes=[
            pltpu.VMEM((FFN_SUBS, d // LANES, sub_rows, LANES), F32),
            pltpu.VMEM((FFN_SUBS, sub_rows, d), BF16),
            pltpu.VMEM((FFN_SUBS, 2, FFN_PLANES, SUBLANES + sub_rows // FFN_PLANES, fc2), F32),
            pltpu.VMEM((n_chunks, FFN_CONV - 1, SUBLANES, fc2), F32),
            pltpu.VMEM((FFN_SUBS, 2, sub_rows, fc2 // 2), BF16),
            pltpu.VMEM((FFN_SUBS, sub_rows, d), F32),
        ],
        compiler_params=_cparams(("parallel", "arbitrary")),
        name="conv_ffn",
    )(x3, g_in, g_out, w_up, taps, w_down)


def _ffn_taps(conv_w, conv_b):
    taps = jnp.concatenate([conv_w, conv_b[None, :]], axis=0)
    taps = jnp.concatenate([taps[:, :D_FF], 0.5 * taps[:, D_FF:]], axis=1)
    return jnp.pad(taps, ((0, SUBLANES - taps.shape[0]), (0, 0)))


def kernel(x, norm_gains, even_w_in, hgrn_lb_logits, hgrn_norm, rg_conv_w, rg_conv_b, rg_wa, rg_ba,
           rg_wx, rg_bx, rg_lambda, even_w_out, odd_w_in, fox_f_bias, odd_w_out,
           ffn_w_up, ffn_conv_w, ffn_conv_b, ffn_w_down):
    b, t, d = x.shape
    n = b * t
    x2 = x.reshape(n, d)
    for l in range(DEPTH):
        g = norm_gains[l]
        if l % 2 == 0:
            e = l // 2
            w_in = even_w_in[e]
            a2 = 2 * A_WIDTH
            w_main = jnp.concatenate([w_in[:, :A_WIDTH], w_in[:, a2:]], axis=1).astype(BF16)
            w_gates = jnp.concatenate(
                [_block_diag_dense(rg_wa[e]), _block_diag_dense(rg_wx[e])], axis=1).astype(BF16)
            b_gates = jnp.concatenate([rg_ba[e], rg_bx[e]])[None, :]
            x2 = _even_mixer(x2.reshape(b, t, d), g[0:1], w_in[:, A_WIDTH:a2].astype(BF16), w_main,
                             hgrn_lb_logits, hgrn_norm[e][None, :],
                             rg_conv_w[e], rg_conv_b[e][None, :], w_gates, b_gates,
                             rg_lambda[e][None, :], even_w_out[e].astype(BF16), g[1:2], l).reshape(n, d)
        else:
            o = l // 2
            w_in = odd_w_in[o]
            wqt = w_in[:, :d].T.astype(BF16)
            wk = w_in[:, d:2 * d].astype(BF16)
            wvt = w_in[:, 2 * d:3 * d].T.astype(BF16)
            wf = jnp.pad(w_in[:, 3 * d:], ((0, 0), (0, LANES - C_HEADS))).astype(BF16)
            qt, k, vt, f = _odd_inproj(x2, g[0:1], wqt, wk, wvt, wf)
            f_tb = f.reshape(b, t, LANES)[:, :, :C_HEADS].transpose(1, 0, 2).reshape(t, b * C_HEADS)
            bias_row = jnp.tile(fox_f_bias[o], b)[None, :]
            c_tb = _fox_cumsum(f_tb, bias_row)
            group = 2 * ATT_PAIRS
            c4 = c_tb.reshape(t, b, C_HEADS // group, group).transpose(1, 2, 0, 3)
            att = _fox_attn(qt, k.reshape(b, t, d), vt, c4)
            x2 = _outproj([att.reshape(n, d)], [odd_w_out[o].astype(BF16)], x2, g[1:2])
        x2 = _ffn(x2.reshape(b, t, d), g[2:3], g[3:4], ffn_w_up[l].astype(BF16),
                  _ffn_taps(ffn_conv_w[l], ffn_conv_b[l]), ffn_w_down[l].astype(BF16)).reshape(n, d)
    return x2.reshape(b, t, d)
```

```python
import functools

import jax
import jax.numpy as jnp
from jax import lax
from jax.experimental import pallas as pl
from jax.experimental.pallas import tpu as pltpu

F32 = jnp.float32
BF16 = jnp.bfloat16

D_MODEL = 1024
DEPTH = 2
A_HEADS = 4
A_DIM = 128
A_WIDTH = A_HEADS * A_DIM
HGRN_CHUNK = 64
B_WIDTH = D_MODEL - A_WIDTH
B_CONV = 4
RG_C = 8.0
C_HEADS = 16
C_HEAD_DIM = D_MODEL // C_HEADS
D_FF = 2816
FFN_CONV = 3
EPS = 1e-6

LANES = 128
SUBLANES = 8
VMEM_LIMIT = 56 * 1024 * 1024

ROW_TILE = 1024
SEQ_TILE = 512
MIX_SUB = 256
RG_ROWS = 64
FFN_TILE = 512
FF_CHUNK = 256
FFN_ROWS = 128
ATT_TILE = 256
ATT_PAIRS = 8
NEG_BIG = -1e30

LOG2E = 1.4426950408889634
C_PARTS = 3
SUM_ROWS = 16
GELU_C0 = 0.7978845608028654
GELU_C1 = 0.044715


def _cparams(semantics):
    return pltpu.CompilerParams(dimension_semantics=semantics, vmem_limit_bytes=VMEM_LIMIT)


def _rms_rows(x, gain):
    ms = jnp.mean(x * x, axis=-1, keepdims=True)
    return x * lax.rsqrt(ms + EPS) * gain


def _gelu_tanh_x2(x):
    inner = x * (GELU_C0 + (GELU_C0 * GELU_C1) * (x * x))
    return x * (1.0 + jnp.tanh(inner))


def _gelu_tanh(x):
    return 0.5 * _gelu_tanh_x2(x)


def _sigmoid(x):
    return 1.0 / (1.0 + jnp.exp(-x))


def _dot(a, b):
    return jnp.dot(a, b, preferred_element_type=F32)


def _dot_nt(a, b):
    return lax.dot_general(a, b, (((1,), (1,)), ((), ())), preferred_element_type=F32)


def _dot_tn(a, b):
    return lax.dot_general(a, b, (((0,), (0,)), ((), ())), preferred_element_type=F32)


def _interleave(*streams):
    live = list(streams)
    while live:
        for stream in list(live):
            if next(stream, _DONE) is _DONE:
                live.remove(stream)


_DONE = object()


def _split_bf16(x):
    hi = x.astype(BF16)
    lo = (x - hi.astype(F32)).astype(BF16)
    return hi, lo


def _odd_inproj_kernel(x_ref, g_ref, wqt_ref, wk_ref, wvt_ref, wf_ref, qt_ref, k_ref, vt_ref, f_ref):
    h = _rms_rows(x_ref[...], g_ref[...]).astype(BF16)
    qt_ref[...] = (_dot_nt(wqt_ref[...], h) * (LOG2E * C_HEAD_DIM ** -0.5)).astype(BF16)
    k_ref[...] = _dot(h, wk_ref[...]).astype(BF16)
    vt_ref[...] = _dot_nt(wvt_ref[...], h).astype(BF16)
    f_ref[...] = _dot(h, wf_ref[...])


def _odd_inproj(x2, gain, wqt_bf16, wk_bf16, wvt_bf16, wf_bf16):
    n, d = x2.shape
    row = lambda i: (i, 0)
    col = lambda i: (0, i)
    fixed = lambda i: (0, 0)
    return pl.pallas_call(
        _odd_inproj_kernel,
        grid=(n // ROW_TILE,),
        in_specs=[
            pl.BlockSpec((ROW_TILE, d), row),
            pl.BlockSpec((1, d), fixed),
            pl.BlockSpec((d, d), fixed),
            pl.BlockSpec((d, d), fixed),
            pl.BlockSpec((d, d), fixed),
            pl.BlockSpec((d, LANES), fixed),
        ],
        out_specs=[
            pl.BlockSpec((d, ROW_TILE), col),
            pl.BlockSpec((ROW_TILE, d), row),
            pl.BlockSpec((d, ROW_TILE), col),
            pl.BlockSpec((ROW_TILE, LANES), row),
        ],
        out_shape=[
            jax.ShapeDtypeStruct((d, n), BF16),
            jax.ShapeDtypeStruct((n, d), BF16),
            jax.ShapeDtypeStruct((d, n), BF16),
            jax.ShapeDtypeStruct((n, LANES), F32),
        ],
        compiler_params=_cparams(("parallel",)),
        name="odd_inproj",
    )(x2, gain, wqt_bf16, wk_bf16, wvt_bf16, wf_bf16)


def _outproj_kernel(a_ref, w_ref, x_ref, g_ref, o_ref):
    o_ref[...] = x_ref[...] + _rms_rows(_dot(a_ref[...], w_ref[...]), g_ref[...])


def _outproj(a_bf16, w_bf16, x2, gain):
    n, d = x2.shape
    row = lambda i: (i, 0)
    fixed = lambda i: (0, 0)
    return pl.pallas_call(
        _outproj_kernel,
        grid=(n // ROW_TILE,),
        in_specs=[pl.BlockSpec((ROW_TILE, a_bf16.shape[1]), row), pl.BlockSpec(w_bf16.shape, fixed),
                  pl.BlockSpec((ROW_TILE, d), row), pl.BlockSpec((1, d), fixed)],
        out_specs=pl.BlockSpec((ROW_TILE, d), row),
        out_shape=jax.ShapeDtypeStruct((n, d), F32),
        compiler_params=_cparams(("parallel",)),
        name="outproj",
    )(a_bf16, w_bf16, x2, gain)


def _hgrn_body(q, f_logit, v, g, lbz_ref, gain_ref, mix_ref, st_ref, *, layer):
    tt = q.shape[0]

    z = lbz_ref[...]
    ez = jnp.exp(z - jnp.max(z, axis=0, keepdims=True))
    lb = jnp.sum(ez[:layer + 1], axis=0, keepdims=True) / jnp.sum(ez, axis=0, keepdims=True)

    forget = lb + (1.0 - lb) * _sigmoid(f_logit)
    logf = jnp.log(forget)

    r = lax.broadcasted_iota(jnp.int32, (tt, tt), 0)
    c = lax.broadcasted_iota(jnp.int32, (tt, tt), 1)
    shift = HGRN_CHUNK.bit_length() - 1
    same_chunk = lax.shift_right_logical(r, shift) == lax.shift_right_logical(c, shift)
    tri = jnp.where((c <= r) & same_chunk, 1.0, 0.0).astype(BF16)
    hi, lo = _split_bf16(logf)
    bcum = _dot(tri, hi) + _dot(tri, lo)

    qs = q.astype(F32)
    qs = qs * _sigmoid(qs)
    kk = 1.0 - forget
    gate = _sigmoid(g.astype(F32))
    gain = gain_ref[...]

    cr = lax.broadcasted_iota(jnp.int32, (HGRN_CHUNK, HGRN_CHUNK), 0)
    cc = lax.broadcasted_iota(jnp.int32, (HGRN_CHUNK, HGRN_CHUNK), 1)
    causal = cc <= cr
    yield

    for ch in range(tt // HGRN_CHUNK):
        rows = slice(ch * HGRN_CHUNK, (ch + 1) * HGRN_CHUNK)
        bc = bcum[rows]
        b_last = bc[HGRN_CHUNK - 1:HGRN_CHUNK]
        q_dec = (qs[rows] * jnp.exp(bc)).astype(BF16)
        k_intra = (kk[rows] * jnp.exp(-bc)).astype(BF16)
        k_upd = (kk[rows] * jnp.exp(b_last - bc)).astype(BF16)
        dec = jnp.exp(b_last)
        v_bf = v[rows]
        outs = []
        for hd in range(A_HEADS):
            cols = slice(hd * A_DIM, (hd + 1) * A_DIM)
            st = st_ref[hd]
            scores = jnp.where(causal, _dot_nt(q_dec[:, cols], k_intra[:, cols]), 0.0)
            o = _dot(scores.astype(BF16), v_bf[:, cols]) + _dot_nt(q_dec[:, cols], st.astype(BF16))
            st_ref[hd] = dec[:, cols] * st + _dot_tn(v_bf[:, cols], k_upd[:, cols])
            o = o * lax.rsqrt(jnp.mean(o * o, axis=-1, keepdims=True) + EPS)
            outs.append(o)
        o_all = jnp.concatenate(outs, axis=-1)
        mix_ref[rows, 0:A_WIDTH] = (o_all * gain * gate[rows]).astype(mix_ref.dtype)
        yield


def _rglru_body(x_br, y_br, cw_ref, cb_ref, wg_ref, bg_ref, lam_ref, mix_ref,
                xext_ref, hprev_ref):
    tt = x_br.shape[0]
    halo = SUBLANES
    x = x_br.astype(F32)
    xext_ref[halo:halo + tt, :] = x
    cw = cw_ref[...]
    xf = cb_ref[...] + cw[B_CONV - 1:B_CONV] * x
    for j in range(1, B_CONV):
        xf = xf + cw[B_CONV - 1 - j:B_CONV - j] * xext_ref[halo - j:halo - j + tt, :]
    xext_ref[0:halo, :] = x[tt - halo:tt]

    gates = _dot(xf.astype(BF16), wg_ref[...]) + bg_ref[...]
    lam = lam_ref[...]
    softplus_neg = jnp.maximum(-lam, 0.0) + jnp.log(1.0 + jnp.exp(-jnp.abs(lam)))
    yield

    groups = RG_ROWS // SUBLANES
    row = lax.broadcasted_iota(jnp.int32, (groups, SUBLANES, B_WIDTH), 1)
    carry = hprev_ref[...]
    for r0 in range(0, tt, RG_ROWS):
        rows = slice(r0, r0 + RG_ROWS)
        r = _sigmoid(gates[rows, :B_WIDTH])
        i = _sigmoid(gates[rows, B_WIDTH:])
        log_a = (-RG_C) * r * softplus_neg
        a = jnp.exp(log_a)
        u = jnp.sqrt(1.0 - jnp.exp(2.0 * log_a)) * (i * xf[rows])
        a3 = a.reshape(groups, SUBLANES, B_WIDTH)
        u3 = u.reshape(groups, SUBLANES, B_WIDTH)
        d = 1
        while d < SUBLANES:
            keep = row >= d
            a_sh = jnp.where(keep, pltpu.roll(a3, d, 1), 1.0)
            u_sh = jnp.where(keep, pltpu.roll(u3, d, 1), 0.0)
            u3 = u3 + a3 * u_sh
            a3 = a3 * a_sh
            d *= 2
        hs = []
        for gi in range(groups):
            hg = a3[gi] * carry + u3[gi]
            hs.append(hg)
            carry = hg[SUBLANES - 1:SUBLANES]
        h = jnp.concatenate(hs, axis=0)
        mix_ref[rows, A_WIDTH:D_MODEL] = (h * _gelu_tanh(y_br[rows].astype(F32))).astype(mix_ref.dtype)
        yield
    hprev_ref[...] = carry


def _even_mixer_kernel(x_ref, gin_ref, wf_ref, win_ref, lbz_ref, gain_ref,
                       cw_ref, cb_ref, wg_ref, bg_ref, lam_ref, wout_ref, gout_ref,
                       o_ref, pf_ref, proj_ref, mix_ref, st_ref, xext_ref, hprev_ref, *, layer):
    @pl.when(pl.program_id(1) == 0)
    def _():
        st_ref[...] = jnp.zeros_like(st_ref)
        xext_ref[0:SUBLANES, :] = jnp.zeros((SUBLANES, B_WIDTH), F32)
        hprev_ref[...] = jnp.zeros_like(hprev_ref)

    gin = gin_ref[...]
    n_sub = x_ref.shape[1] // MIX_SUB

    def project(s):
        h = _rms_rows(x_ref[0, s * MIX_SUB:(s + 1) * MIX_SUB, :], gin).astype(BF16)
        pf_ref[s] = _dot(h, wf_ref[...])
        yield
        for blk in range(proj_ref.shape[1]):
            proj_ref[s, blk] = _dot(h, win_ref[:, blk * A_WIDTH:(blk + 1) * A_WIDTH]).astype(BF16)
            yield

    _interleave(project(0))
    for s in range(n_sub):
        f_logit = pf_ref[s]
        q, v, g, x_br, y_br = [proj_ref[s, blk] for blk in range(proj_ref.shape[1])]
        rows = slice(s * MIX_SUB, (s + 1) * MIX_SUB)
        mix_sub = mix_ref.at[rows]
        streams = [
            _hgrn_body(q, f_logit, v, g, lbz_ref, gain_ref, mix_sub, st_ref, layer=layer),
            _rglru_body(x_br, y_br, cw_ref, cb_ref, wg_ref, bg_ref, lam_ref, mix_sub, xext_ref, hprev_ref)]
        if s + 1 < n_sub:
            streams.append(project(s + 1))
        _interleave(*streams)
        mixed = _dot(mix_ref[rows, :], wout_ref[...])
        o_ref[0, rows, :] = x_ref[0, rows, :] + _rms_rows(mixed, gout_ref[...])


def _even_mixer(x3, g_in, w_f, w_main, lb_logits, norm_gain, conv_w, conv_b, w_gates, b_gates,
                lam, w_out, g_out, layer):
    b, t, d = x3.shape
    fixed = lambda bi, ti: (0, 0)
    full = lambda a: pl.BlockSpec(a.shape, fixed)
    resident = lambda a: pl.BlockSpec(a.shape, fixed, pipeline_mode=pl.Buffered(1))
    row_tile = pl.BlockSpec((1, SEQ_TILE, d), lambda bi, ti: (bi, ti, 0))
    return pl.pallas_call(
        functools.partial(_even_mixer_kernel, layer=layer),
        grid=(b, t // SEQ_TILE),
        in_specs=[row_tile, full(g_in), resident(w_f), resident(w_main),
                  full(lb_logits), full(norm_gain), full(conv_w), full(conv_b),
                  resident(w_gates), full(b_gates), full(lam), resident(w_out), full(g_out)],
        out_specs=row_tile,
        out_shape=jax.ShapeDtypeStruct((b, t, d), F32),
        scratch_shapes=[pltpu.VMEM((SEQ_TILE // MIX_SUB, MIX_SUB, A_WIDTH), F32),
                        pltpu.VMEM((SEQ_TILE // MIX_SUB, w_main.shape[1] // A_WIDTH, MIX_SUB, A_WIDTH), BF16),
                        pltpu.VMEM((SEQ_TILE, d), BF16),
                        pltpu.VMEM((A_HEADS, A_DIM, A_DIM), F32),
                        pltpu.VMEM((SEQ_TILE + SUBLANES, B_WIDTH), F32),
                        pltpu.VMEM((1, B_WIDTH), F32)],
        compiler_params=_cparams(("parallel", "arbitrary")),
        name="even_mixer",
    )(x3, g_in, w_f, w_main, lb_logits, norm_gain, conv_w, conv_b, w_gates, b_gates, lam,
      w_out, g_out)


def _block_diag_dense(w):
    nb, bd, _ = w.shape
    eye = jnp.eye(nb, dtype=w.dtype)
    return jnp.einsum('nij,nm->nimj', w, eye).reshape(nb * bd, nb * bd)


def _fox_cumsum_kernel(f_ref, b_ref, o_ref, *, blk):
    t, w = f_ref.shape
    z = f_ref[...] + b_ref[...]
    logsig = jnp.minimum(z, 0.0) - jnp.log(1.0 + jnp.exp(-jnp.abs(z)))
    r = lax.broadcasted_iota(jnp.int32, (blk, blk), 0)
    c = lax.broadcasted_iota(jnp.int32, (blk, blk), 1)
    tri = jnp.where(c <= r, 1.0, 0.0).astype(BF16)
    carry = jnp.zeros((1, w), F32)
    for s in range(0, t, blk):
        x = logsig[s:s + blk]
        hi = x.astype(BF16)
        r1 = x - hi.astype(F32)
        mid = r1.astype(BF16)
        lo = (r1 - mid.astype(F32)).astype(BF16)
        cs = _dot(tri, hi) + _dot(tri, mid) + _dot(tri, lo) + carry
        o_ref[s:s + blk, :] = cs * (-LOG2E)
        carry = cs[blk - 1:blk]


def _fox_cumsum(f_tb, bias_row):
    t, w = f_tb.shape
    return pl.pallas_call(
        functools.partial(_fox_cumsum_kernel, blk=256),
        out_shape=jax.ShapeDtypeStruct((t, w), F32),
        compiler_params=pltpu.CompilerParams(vmem_limit_bytes=VMEM_LIMIT),
        name="fox_cumsum",
    )(f_tb, bias_row)


def _fox_attn_kernel(q_ref, k_ref, vt_ref, c_ref, o_ref, ka_ref, qs_ref, m_ref, acc_ref):
    tq = q_ref.shape[1]
    tk = tq
    n_pairs = q_ref.shape[0] // LANES
    qi = pl.program_id(2)
    dim = lax.broadcasted_iota(jnp.int32, (LANES, tq), 0)
    low = dim < C_HEAD_DIM
    key_row = lax.broadcasted_iota(jnp.int32, (tk, tq), 0)
    qry_col = lax.broadcasted_iota(jnp.int32, (tk, tq), 1)
    diag_ok = key_row <= qry_col

    @pl.when(qi == 0)
    def _():
        rest = c_ref[0, 0]
        n_heads = rest.shape[1]
        pieces = []
        for _ in range(C_PARTS):
            part = rest.astype(BF16)
            pieces.append(part)
            rest = rest - part.astype(F32)
        c_parts = jnp.concatenate(pieces, axis=1)
        src = lax.broadcasted_iota(jnp.int32, (C_PARTS * n_heads, LANES), 0)
        dst = lax.broadcasted_iota(jnp.int32, (C_PARTS * n_heads, LANES), 1)
        for p in range(n_pairs):
            pick = None
            for hd in range(2):
                for part in range(C_PARTS):
                    hit = (src == part * n_heads + 2 * p + hd) & (dst == hd * C_PARTS + part)
                    pick = hit if pick is None else (pick | hit)
            place = jnp.where(pick, 1.0, 0.0).astype(BF16)
            ka_ref[p, :, 0:LANES] = k_ref[0, :, p * LANES:(p + 1) * LANES]
            ka_ref[p, :, LANES:2 * LANES] = _dot(c_parts, place).astype(BF16)

    for p in range(n_pairs):
        q2t = q_ref[p * LANES:(p + 1) * LANES, :]
        zero = jnp.zeros_like(q2t)
        qs_ref[p, 0:LANES, 0:tq] = jnp.where(low, q2t, zero)
        qs_ref[p, 0:LANES, tq:2 * tq] = jnp.where(low, zero, q2t)
        qs_ref[p, LANES:2 * LANES, 0:tq] = jnp.where(dim < C_PARTS, 1.0, 0.0).astype(BF16)
        qs_ref[p, LANES:2 * LANES, tq:2 * tq] = jnp.where(
            (dim >= C_PARTS) & (dim < 2 * C_PARTS), 1.0, 0.0).astype(BF16)
        m_ref[p] = jnp.full((1, 2 * tq), NEG_BIG, F32)
        acc_ref[p] = jnp.zeros((LANES + SUM_ROWS, 2 * tq), F32)

    ones = jnp.ones((SUM_ROWS, tk), BF16)

    def step(j, masked):
        start = pl.multiple_of(j * tk, tk)
        scores = []
        for p in range(n_pairs):
            scores.append(_dot(ka_ref[p, pl.ds(start, tk), :], qs_ref[p]))
        probs, alphas = [], []
        for p in range(n_pairs):
            s = scores[p]
            if masked:
                s = jnp.concatenate([jnp.where(diag_ok, s[:, 0:tq], NEG_BIG),
                                     jnp.where(diag_ok, s[:, tq:2 * tq], NEG_BIG)], axis=1)
            m_prev = m_ref[p]
            m_new = jnp.maximum(m_prev, jnp.max(s, axis=0, keepdims=True))
            alpha = jnp.exp2(m_prev - m_new)
            pr = jnp.exp2(s - m_new)
            m_ref[p] = m_new
            probs.append(pr.astype(BF16))
            alphas.append(alpha)
        for p in range(n_pairs):
            vtb = jnp.concatenate([vt_ref[p * LANES:(p + 1) * LANES, pl.ds(start, tk)], ones], axis=0)
            acc_ref[p] = alphas[p] * acc_ref[p] + _dot(vtb, probs[p])

    def body(j, carry):
        step(j, False)
        return carry

    lax.fori_loop(0, qi, body, 0)
    step(qi, True)

    for p in range(n_pairs):
        o = acc_ref[p, 0:LANES, :] * (1.0 / acc_ref[p, LANES:LANES + 1, :])
        o = jnp.where(low, o[:, 0:tq], o[:, tq:2 * tq])
        o_ref[0, :, p * LANES:(p + 1) * LANES] = o.T.astype(o_ref.dtype)


def _fox_attn(qt, k3, vt, c4):
    b, t, d = k3.shape
    n_q = t // ATT_TILE
    width = ATT_PAIRS * LANES
    row_state = pltpu.VMEM((ATT_PAIRS, 1, 2 * ATT_TILE), F32)
    return pl.pallas_call(
        _fox_attn_kernel,
        grid=(b, d // width, t // ATT_TILE),
        in_specs=[
            pl.BlockSpec((width, ATT_TILE), lambda bi, g, qi: (g, bi * n_q + qi)),
            pl.BlockSpec((1, t, width), lambda bi, g, qi: (bi, 0, g)),
            pl.BlockSpec((width, t), lambda bi, g, qi: (g, bi)),
            pl.BlockSpec((1, 1, t, 2 * ATT_PAIRS), lambda bi, g, qi: (bi, g, 0, 0)),
        ],
        out_specs=pl.BlockSpec((1, ATT_TILE, width), lambda bi, g, qi: (bi, qi, g)),
        out_shape=jax.ShapeDtypeStruct((b, t, d), BF16),
        scratch_shapes=[pltpu.VMEM((ATT_PAIRS, t, 2 * LANES), BF16),
                        pltpu.VMEM((ATT_PAIRS, 2 * LANES, 2 * ATT_TILE), BF16), row_state,
                        pltpu.VMEM((ATT_PAIRS, LANES + SUM_ROWS, 2 * ATT_TILE), F32)],
        compiler_params=_cparams(("parallel", "parallel", "arbitrary")),
        name="fox_attn",
    )(qt, k3, vt, c4)


def _ffn_kernel(x_ref, g_in_ref, g_out_ref, wup_ref, cw_ref, wdn_ref, o_ref,
                h_ref, ext_ref, prev_ref, act_ref, acc_ref):
    tm = x_ref.shape[1]
    halo = SUBLANES
    fc = FF_CHUNK
    d_ff = wdn_ref.shape[0]
    n_chunks = d_ff // fc

    @pl.when(pl.program_id(1) == 0)
    def _():
        prev_ref[...] = jnp.zeros_like(prev_ref)

    x = x_ref[0]
    h_ref[...] = _rms_rows(x, g_in_ref[...]).astype(BF16)
    acc_ref[...] = jnp.zeros_like(acc_ref)

    def issue_up(j):
        ext = ext_ref.at[j % 2]
        ext[0:halo, :] = prev_ref[j]
        for half in range(2):
            w0 = half * d_ff + j * fc
            ext[halo:halo + tm, half * fc:(half + 1) * fc] = _dot(h_ref[...], wup_ref[:, w0:w0 + fc])

    def issue_down(j):
        acc_ref[...] += _dot(act_ref[j % 2], wdn_ref[j * fc:(j + 1) * fc, :])

    issue_up(0)
    for j in range(n_chunks):
        if j + 1 < n_chunks:
            issue_up(j + 1)
        if j >= 1:
            issue_down(j - 1)
        ext = ext_ref.at[j % 2]
        prev_ref[j] = ext[tm:tm + halo, :]
        cw = jnp.concatenate([cw_ref[:, j * fc:(j + 1) * fc],
                              cw_ref[:, d_ff + j * fc:d_ff + (j + 1) * fc]], axis=1)
        for r0 in range(0, tm, FFN_ROWS):
            base = halo + r0
            conv = cw[FFN_CONV:FFN_CONV + 1] + cw[FFN_CONV - 1:FFN_CONV] * ext[base:base + FFN_ROWS, :]
            for s in range(1, FFN_CONV):
                conv = conv + cw[FFN_CONV - 1 - s:FFN_CONV - s] * ext[base - s:base - s + FFN_ROWS, :]
            act_ref[j % 2, r0:r0 + FFN_ROWS, :] = (_gelu_tanh_x2(conv[:, :fc]) * conv[:, fc:]).astype(BF16)
    issue_down(n_chunks - 1)
    o_ref[0] = x + _rms_rows(acc_ref[...], g_out_ref[...])


def _ffn(x3, g_in, g_out, w_up, taps, w_down):
    b, t, d = x3.shape
    n_chunks = w_down.shape[0] // FF_CHUNK
    fc2 = 2 * FF_CHUNK
    fixed2 = lambda bi, ti: (0, 0)
    return pl.pallas_call(
        _ffn_kernel,
        grid=(b, t // FFN_TILE),
        in_specs=[
            pl.BlockSpec((1, FFN_TILE, d), lambda bi, ti: (bi, ti, 0)),
            pl.BlockSpec((1, d), fixed2),
            pl.BlockSpec((1, d), fixed2),
            pl.BlockSpec(w_up.shape, fixed2, pipeline_mode=pl.Buffered(1)),
            pl.BlockSpec(taps.shape, fixed2, pipeline_mode=pl.Buffered(1)),
            pl.BlockSpec(w_down.shape, fixed2, pipeline_mode=pl.Buffered(1)),
        ],
        out_specs=pl.BlockSpec((1, FFN_TILE, d), lambda bi, ti: (bi, ti, 0)),
        out_shape=jax.ShapeDtypeStruct((b, t, d), F32),
        scratch_shapes=[
            pltpu.VMEM((FFN_TILE, d), BF16),
            pltpu.VMEM((2, FFN_TILE + SUBLANES, fc2), F32),
            pltpu.VMEM((n_chunks, SUBLANES, fc2), F32),
            pltpu.VMEM((2, FFN_TILE, fc2 // 2), BF16),
            pltpu.VMEM((FFN_TILE, d), F32),
        ],
        compiler_params=_cparams(("parallel", "arbitrary")),
        name="conv_ffn",
    )(x3, g_in, g_out, w_up, taps, w_down)


def _ffn_taps(conv_w, conv_b):
    taps = jnp.concatenate([conv_w, conv_b[None, :]], axis=0)
    taps = jnp.concatenate([taps[:, :D_FF], 0.5 * taps[:, D_FF:]], axis=1)
    return jnp.pad(taps, ((0, SUBLANES - taps.shape[0]), (0, 0)))


def kernel(x, norm_gains, even_w_in, hgrn_lb_logits, hgrn_norm, rg_conv_w, rg_conv_b, rg_wa, rg_ba,
           rg_wx, rg_bx, rg_lambda, even_w_out, odd_w_in, fox_f_bias, odd_w_out,
           ffn_w_up, ffn_conv_w, ffn_conv_b, ffn_w_down):
    b, t, d = x.shape
    n = b * t
    x2 = x.reshape(n, d)
    for l in range(DEPTH):
        g = norm_gains[l]
        if l % 2 == 0:
            e = l // 2
            w_in = even_w_in[e]
            a2 = 2 * A_WIDTH
            w_main = jnp.concatenate([w_in[:, :A_WIDTH], w_in[:, a2:]], axis=1).astype(BF16)
            w_gates = jnp.concatenate(
                [_block_diag_dense(rg_wa[e]), _block_diag_dense(rg_wx[e])], axis=1).astype(BF16)
            b_gates = jnp.concatenate([rg_ba[e], rg_bx[e]])[None, :]
            x2 = _even_mixer(x2.reshape(b, t, d), g[0:1], w_in[:, A_WIDTH:a2].astype(BF16), w_main,
                             hgrn_lb_logits, hgrn_norm[e][None, :],
                             rg_conv_w[e], rg_conv_b[e][None, :], w_gates, b_gates,
                             rg_lambda[e][None, :], even_w_out[e].astype(BF16), g[1:2], l).reshape(n, d)
        else:
            o = l // 2
            w_in = odd_w_in[o]
            wqt = w_in[:, :d].T.astype(BF16)
            wk = w_in[:, d:2 * d].astype(BF16)
            wvt = w_in[:, 2 * d:3 * d].T.astype(BF16)
            wf = jnp.pad(w_in[:, 3 * d:], ((0, 0), (0, LANES - C_HEADS))).astype(BF16)
            qt, k, vt, f = _odd_inproj(x2, g[0:1], wqt, wk, wvt, wf)
            f_tb = f.reshape(b, t, LANES)[:, :, :C_HEADS].transpose(1, 0, 2).reshape(t, b * C_HEADS)
            bias_row = jnp.tile(fox_f_bias[o], b)[None, :]
            c_tb = _fox_cumsum(f_tb, bias_row)
            group = 2 * ATT_PAIRS
            c4 = c_tb.reshape(t, b, C_HEADS // group, group).transpose(1, 2, 0, 3)
            att = _fox_attn(qt, k.reshape(b, t, d), vt, c4)
            x2 = _outproj(att.reshape(n, d), odd_w_out[o].astype(BF16), x2, g[1:2])
        x2 = _ffn(x2.reshape(b, t, d), g[2:3], g[3:4], ffn_w_up[l].astype(BF16),
                  _ffn_taps(ffn_conv_w[l], ffn_conv_b[l]), ffn_w_down[l].astype(BF16)).reshape(n, d)
    return x2.reshape(b, t, d)
```

```python
import functools

import jax
import jax.numpy as jnp
from jax import lax
from jax.experimental import pallas as pl
from jax.experimental.pallas import tpu as pltpu

F32 = jnp.float32
BF16 = jnp.bfloat16

D_MODEL = 1024
DEPTH = 2
A_HEADS = 4
A_DIM = 128
A_WIDTH = A_HEADS * A_DIM
HGRN_CHUNK = 64
B_WIDTH = D_MODEL - A_WIDTH
B_CONV = 4
RG_C = 8.0
C_HEADS = 16
C_HEAD_DIM = D_MODEL // C_HEADS
D_FF = 2816
FFN_CONV = 3
EPS = 1e-6

LANES = 128
SUBLANES = 8
VMEM_LIMIT = 56 * 1024 * 1024

ROW_TILE = 1024
SEQ_TILE = 512
MIX_SUB = 256
RG_ROWS = 64
FFN_TILE = 512
FF_CHUNK = 256
FFN_DOWN_GROUP = 2
FFN_ROWS = 128
ATT_TILE = 256
ATT_PAIRS = 8
NEG_BIG = -1e30

LOG2E = 1.4426950408889634
C_PARTS = 3
SUM_ROWS = 16
GELU_C0 = 0.7978845608028654
GELU_C1 = 0.044715


def _cparams(semantics):
    return pltpu.CompilerParams(dimension_semantics=semantics, vmem_limit_bytes=VMEM_LIMIT)


def _rms_rows(x, gain):
    ms = jnp.mean(x * x, axis=-1, keepdims=True)
    return x * lax.rsqrt(ms + EPS) * gain


def _gelu_tanh_x2(x):
    inner = x * (GELU_C0 + (GELU_C0 * GELU_C1) * (x * x))
    return x * (1.0 + jnp.tanh(inner))


def _gelu_tanh(x):
    return 0.5 * _gelu_tanh_x2(x)


def _sigmoid(x):
    return 1.0 / (1.0 + jnp.exp(-x))


def _dot(a, b):
    return jnp.dot(a, b, preferred_element_type=F32)


def _dot_nt(a, b):
    return lax.dot_general(a, b, (((1,), (1,)), ((), ())), preferred_element_type=F32)


def _dot_tn(a, b):
    return lax.dot_general(a, b, (((0,), (0,)), ((), ())), preferred_element_type=F32)


def _interleave(*streams):
    live = list(streams)
    while live:
        for stream in list(live):
            if next(stream, _DONE) is _DONE:
                live.remove(stream)


_DONE = object()


def _split_bf16(x):
    hi = x.astype(BF16)
    lo = (x - hi.astype(F32)).astype(BF16)
    return hi, lo


def _odd_inproj_kernel(x_ref, g_ref, wqt_ref, wk_ref, wvt_ref, wf_ref, qt_ref, k_ref, vt_ref, f_ref):
    h = _rms_rows(x_ref[...], g_ref[...]).astype(BF16)
    qt_ref[...] = (_dot_nt(wqt_ref[...], h) * (LOG2E * C_HEAD_DIM ** -0.5)).astype(BF16)
    k_ref[...] = _dot(h, wk_ref[...]).astype(BF16)
    vt_ref[...] = _dot_nt(wvt_ref[...], h).astype(BF16)
    f_ref[...] = _dot(h, wf_ref[...])


def _odd_inproj(x2, gain, wqt_bf16, wk_bf16, wvt_bf16, wf_bf16):
    n, d = x2.shape
    row = lambda i: (i, 0)
    col = lambda i: (0, i)
    fixed = lambda i: (0, 0)
    return pl.pallas_call(
        _odd_inproj_kernel,
        grid=(n // ROW_TILE,),
        in_specs=[
            pl.BlockSpec((ROW_TILE, d), row),
            pl.BlockSpec((1, d), fixed),
            pl.BlockSpec((d, d), fixed),
            pl.BlockSpec((d, d), fixed),
            pl.BlockSpec((d, d), fixed),
            pl.BlockSpec((d, LANES), fixed),
        ],
        out_specs=[
            pl.BlockSpec((d, ROW_TILE), col),
            pl.BlockSpec((ROW_TILE, d), row),
            pl.BlockSpec((d, ROW_TILE), col),
            pl.BlockSpec((ROW_TILE, LANES), row),
        ],
        out_shape=[
            jax.ShapeDtypeStruct((d, n), BF16),
            jax.ShapeDtypeStruct((n, d), BF16),
            jax.ShapeDtypeStruct((d, n), BF16),
            jax.ShapeDtypeStruct((n, LANES), F32),
        ],
        compiler_params=_cparams(("parallel",)),
        name="odd_inproj",
    )(x2, gain, wqt_bf16, wk_bf16, wvt_bf16, wf_bf16)


def _outproj_kernel(a_ref, w_ref, x_ref, g_ref, o_ref):
    o_ref[...] = x_ref[...] + _rms_rows(_dot(a_ref[...], w_ref[...]), g_ref[...])


def _outproj(a_bf16, w_bf16, x2, gain):
    n, d = x2.shape
    row = lambda i: (i, 0)
    fixed = lambda i: (0, 0)
    return pl.pallas_call(
        _outproj_kernel,
        grid=(n // ROW_TILE,),
        in_specs=[pl.BlockSpec((ROW_TILE, a_bf16.shape[1]), row), pl.BlockSpec(w_bf16.shape, fixed),
                  pl.BlockSpec((ROW_TILE, d), row), pl.BlockSpec((1, d), fixed)],
        out_specs=pl.BlockSpec((ROW_TILE, d), row),
        out_shape=jax.ShapeDtypeStruct((n, d), F32),
        compiler_params=_cparams(("parallel",)),
        name="outproj",
    )(a_bf16, w_bf16, x2, gain)


def _hgrn_body(q, f_logit, v, g, lbz_ref, gain_ref, mix_ref, st_ref, *, layer):
    tt = q.shape[0]

    z = lbz_ref[...]
    ez = jnp.exp(z - jnp.max(z, axis=0, keepdims=True))
    lb = jnp.sum(ez[:layer + 1], axis=0, keepdims=True) / jnp.sum(ez, axis=0, keepdims=True)

    forget = lb + (1.0 - lb) * _sigmoid(f_logit)
    logf = jnp.log(forget)

    r = lax.broadcasted_iota(jnp.int32, (tt, tt), 0)
    c = lax.broadcasted_iota(jnp.int32, (tt, tt), 1)
    shift = HGRN_CHUNK.bit_length() - 1
    same_chunk = lax.shift_right_logical(r, shift) == lax.shift_right_logical(c, shift)
    tri = jnp.where((c <= r) & same_chunk, 1.0, 0.0).astype(BF16)
    hi, lo = _split_bf16(logf)
    bcum = _dot(tri, hi) + _dot(tri, lo)

    qs = q.astype(F32)
    qs = qs * _sigmoid(qs)
    kk = 1.0 - forget
    gate = _sigmoid(g.astype(F32))
    gain = gain_ref[...]

    cr = lax.broadcasted_iota(jnp.int32, (HGRN_CHUNK, HGRN_CHUNK), 0)
    cc = lax.broadcasted_iota(jnp.int32, (HGRN_CHUNK, HGRN_CHUNK), 1)
    causal = cc <= cr
    yield

    for ch in range(tt // HGRN_CHUNK):
        rows = slice(ch * HGRN_CHUNK, (ch + 1) * HGRN_CHUNK)
        bc = bcum[rows]
        b_last = bc[HGRN_CHUNK - 1:HGRN_CHUNK]
        q_dec = (qs[rows] * jnp.exp(bc)).astype(BF16)
        k_intra = (kk[rows] * jnp.exp(-bc)).astype(BF16)
        k_upd = (kk[rows] * jnp.exp(b_last - bc)).astype(BF16)
        dec = jnp.exp(b_last)
        v_bf = v[rows]
        outs = []
        for hd in range(A_HEADS):
            cols = slice(hd * A_DIM, (hd + 1) * A_DIM)
            st = st_ref[hd]
            scores = jnp.where(causal, _dot_nt(q_dec[:, cols], k_intra[:, cols]), 0.0)
            o = _dot(scores.astype(BF16), v_bf[:, cols]) + _dot_nt(q_dec[:, cols], st.astype(BF16))
            st_ref[hd] = dec[:, cols] * st + _dot_tn(v_bf[:, cols], k_upd[:, cols])
            o = o * lax.rsqrt(jnp.mean(o * o, axis=-1, keepdims=True) + EPS)
            outs.append(o)
        o_all = jnp.concatenate(outs, axis=-1)
        mix_ref[rows, 0:A_WIDTH] = (o_all * gain * gate[rows]).astype(mix_ref.dtype)
        yield


def _rglru_body(x_br, y_br, cw_ref, cb_ref, wg_ref, bg_ref, lam_ref, mix_ref,
                xext_ref, hprev_ref):
    tt = x_br.shape[0]
    halo = SUBLANES
    x = x_br.astype(F32)
    xext_ref[halo:halo + tt, :] = x
    cw = cw_ref[...]
    xf = cb_ref[...] + cw[B_CONV - 1:B_CONV] * x
    for j in range(1, B_CONV):
        xf = xf + cw[B_CONV - 1 - j:B_CONV - j] * xext_ref[halo - j:halo - j + tt, :]
    xext_ref[0:halo, :] = x[tt - halo:tt]

    gates = _dot(xf.astype(BF16), wg_ref[...]) + bg_ref[...]
    lam = lam_ref[...]
    softplus_neg = jnp.maximum(-lam, 0.0) + jnp.log(1.0 + jnp.exp(-jnp.abs(lam)))
    yield

    groups = RG_ROWS // SUBLANES
    row = lax.broadcasted_iota(jnp.int32, (groups, SUBLANES, B_WIDTH), 1)
    carry = hprev_ref[...]
    for r0 in range(0, tt, RG_ROWS):
        rows = slice(r0, r0 + RG_ROWS)
        r = _sigmoid(gates[rows, :B_WIDTH])
        i = _sigmoid(gates[rows, B_WIDTH:])
        log_a = (-RG_C) * r * softplus_neg
        a = jnp.exp(log_a)
        u = jnp.sqrt(1.0 - jnp.exp(2.0 * log_a)) * (i * xf[rows])
        a3 = a.reshape(groups, SUBLANES, B_WIDTH)
        u3 = u.reshape(groups, SUBLANES, B_WIDTH)
        d = 1
        while d < SUBLANES:
            keep = row >= d
            a_sh = jnp.where(keep, pltpu.roll(a3, d, 1), 1.0)
            u_sh = jnp.where(keep, pltpu.roll(u3, d, 1), 0.0)
            u3 = u3 + a3 * u_sh
            a3 = a3 * a_sh
            d *= 2
        hs = []
        for gi in range(groups):
            hg = a3[gi] * carry + u3[gi]
            hs.append(hg)
            carry = hg[SUBLANES - 1:SUBLANES]
        h = jnp.concatenate(hs, axis=0)
        mix_ref[rows, A_WIDTH:D_MODEL] = (h * _gelu_tanh(y_br[rows].astype(F32))).astype(mix_ref.dtype)
        yield
    hprev_ref[...] = carry


def _even_mixer_kernel(x_ref, gin_ref, wf_ref, win_ref, lbz_ref, gain_ref,
                       cw_ref, cb_ref, wg_ref, bg_ref, lam_ref, wout_ref, gout_ref,
                       o_ref, pf_ref, proj_ref, mix_ref, st_ref, xext_ref, hprev_ref, *, layer):
    @pl.when(pl.program_id(1) == 0)
    def _():
        st_ref[...] = jnp.zeros_like(st_ref)
        xext_ref[0:SUBLANES, :] = jnp.zeros((SUBLANES, B_WIDTH), F32)
        hprev_ref[...] = jnp.zeros_like(hprev_ref)

    gin = gin_ref[...]
    n_sub = x_ref.shape[1] // MIX_SUB

    def project(s):
        h = _rms_rows(x_ref[0, s * MIX_SUB:(s + 1) * MIX_SUB, :], gin).astype(BF16)
        pf_ref[s] = _dot(h, wf_ref[...])
        yield
        for blk in range(proj_ref.shape[1]):
            proj_ref[s, blk] = _dot(h, win_ref[:, blk * A_WIDTH:(blk + 1) * A_WIDTH]).astype(BF16)
            yield

    _interleave(project(0))
    for s in range(n_sub):
        f_logit = pf_ref[s]
        q, v, g, x_br, y_br = [proj_ref[s, blk] for blk in range(proj_ref.shape[1])]
        rows = slice(s * MIX_SUB, (s + 1) * MIX_SUB)
        mix_sub = mix_ref.at[rows]
        streams = [
            _hgrn_body(q, f_logit, v, g, lbz_ref, gain_ref, mix_sub, st_ref, layer=layer),
            _rglru_body(x_br, y_br, cw_ref, cb_ref, wg_ref, bg_ref, lam_ref, mix_sub, xext_ref, hprev_ref)]
        if s + 1 < n_sub:
            streams.append(project(s + 1))
        _interleave(*streams)
        mixed = _dot(mix_ref[rows, :], wout_ref[...])
        o_ref[0, rows, :] = x_ref[0, rows, :] + _rms_rows(mixed, gout_ref[...])


def _even_mixer(x3, g_in, w_f, w_main, lb_logits, norm_gain, conv_w, conv_b, w_gates, b_gates,
                lam, w_out, g_out, layer):
    b, t, d = x3.shape
    fixed = lambda bi, ti: (0, 0)
    full = lambda a: pl.BlockSpec(a.shape, fixed)
    resident = lambda a: pl.BlockSpec(a.shape, fixed, pipeline_mode=pl.Buffered(1))
    row_tile = pl.BlockSpec((1, SEQ_TILE, d), lambda bi, ti: (bi, ti, 0))
    return pl.pallas_call(
        functools.partial(_even_mixer_kernel, layer=layer),
        grid=(b, t // SEQ_TILE),
        in_specs=[row_tile, full(g_in), resident(w_f), resident(w_main),
                  full(lb_logits), full(norm_gain), full(conv_w), full(conv_b),
                  resident(w_gates), full(b_gates), full(lam), resident(w_out), full(g_out)],
        out_specs=row_tile,
        out_shape=jax.ShapeDtypeStruct((b, t, d), F32),
        scratch_shapes=[pltpu.VMEM((SEQ_TILE // MIX_SUB, MIX_SUB, A_WIDTH), F32),
                        pltpu.VMEM((SEQ_TILE // MIX_SUB, w_main.shape[1] // A_WIDTH, MIX_SUB, A_WIDTH), BF16),
                        pltpu.VMEM((SEQ_TILE, d), BF16),
                        pltpu.VMEM((A_HEADS, A_DIM, A_DIM), F32),
                        pltpu.VMEM((SEQ_TILE + SUBLANES, B_WIDTH), F32),
                        pltpu.VMEM((1, B_WIDTH), F32)],
        compiler_params=_cparams(("parallel", "arbitrary")),
        name="even_mixer",
    )(x3, g_in, w_f, w_main, lb_logits, norm_gain, conv_w, conv_b, w_gates, b_gates, lam,
      w_out, g_out)


def _block_diag_dense(w):
    nb, bd, _ = w.shape
    eye = jnp.eye(nb, dtype=w.dtype)
    return jnp.einsum('nij,nm->nimj', w, eye).reshape(nb * bd, nb * bd)


def _fox_cumsum_kernel(f_ref, b_ref, o_ref, *, blk):
    t, w = f_ref.shape
    z = f_ref[...] + b_ref[...]
    logsig = jnp.minimum(z, 0.0) - jnp.log(1.0 + jnp.exp(-jnp.abs(z)))
    r = lax.broadcasted_iota(jnp.int32, (blk, blk), 0)
    c = lax.broadcasted_iota(jnp.int32, (blk, blk), 1)
    tri = jnp.where(c <= r, 1.0, 0.0).astype(BF16)
    carry = jnp.zeros((1, w), F32)
    for s in range(0, t, blk):
        x = logsig[s:s + blk]
        hi = x.astype(BF16)
        r1 = x - hi.astype(F32)
        mid = r1.astype(BF16)
        lo = (r1 - mid.astype(F32)).astype(BF16)
        cs = _dot(tri, hi) + _dot(tri, mid) + _dot(tri, lo) + carry
        o_ref[s:s + blk, :] = cs * (-LOG2E)
        carry = cs[blk - 1:blk]


def _fox_cumsum(f_tb, bias_row):
    t, w = f_tb.shape
    return pl.pallas_call(
        functools.partial(_fox_cumsum_kernel, blk=256),
        out_shape=jax.ShapeDtypeStruct((t, w), F32),
        compiler_params=pltpu.CompilerParams(vmem_limit_bytes=VMEM_LIMIT),
        name="fox_cumsum",
    )(f_tb, bias_row)


def _fox_attn_kernel(q_ref, k_ref, vt_ref, c_ref, o_ref, ka_ref, qs_ref, m_ref, acc_ref):
    tq = q_ref.shape[1]
    tk = tq
    n_pairs = q_ref.shape[0] // LANES
    qi = pl.program_id(2)
    dim = lax.broadcasted_iota(jnp.int32, (LANES, tq), 0)
    low = dim < C_HEAD_DIM
    key_row = lax.broadcasted_iota(jnp.int32, (tk, tq), 0)
    qry_col = lax.broadcasted_iota(jnp.int32, (tk, tq), 1)
    diag_ok = key_row <= qry_col

    @pl.when(qi == 0)
    def _():
        rest = c_ref[0, 0]
        n_heads = rest.shape[1]
        pieces = []
        for _ in range(C_PARTS):
            part = rest.astype(BF16)
            pieces.append(part)
            rest = rest - part.astype(F32)
        c_parts = jnp.concatenate(pieces, axis=1)
        src = lax.broadcasted_iota(jnp.int32, (C_PARTS * n_heads, LANES), 0)
        dst = lax.broadcasted_iota(jnp.int32, (C_PARTS * n_heads, LANES), 1)
        for p in range(n_pairs):
            pick = None
            for hd in range(2):
                for part in range(C_PARTS):
                    hit = (src == part * n_heads + 2 * p + hd) & (dst == hd * C_PARTS + part)
                    pick = hit if pick is None else (pick | hit)
            place = jnp.where(pick, 1.0, 0.0).astype(BF16)
            ka_ref[p, :, 0:LANES] = k_ref[0, :, p * LANES:(p + 1) * LANES]
            ka_ref[p, :, LANES:2 * LANES] = _dot(c_parts, place).astype(BF16)

    for p in range(n_pairs):
        q2t = q_ref[p * LANES:(p + 1) * LANES, :]
        zero = jnp.zeros_like(q2t)
        qs_ref[p, 0:LANES, 0:tq] = jnp.where(low, q2t, zero)
        qs_ref[p, 0:LANES, tq:2 * tq] = jnp.where(low, zero, q2t)
        qs_ref[p, LANES:2 * LANES, 0:tq] = jnp.where(dim < C_PARTS, 1.0, 0.0).astype(BF16)
        qs_ref[p, LANES:2 * LANES, tq:2 * tq] = jnp.where(
            (dim >= C_PARTS) & (dim < 2 * C_PARTS), 1.0, 0.0).astype(BF16)
        m_ref[p] = jnp.full((1, 2 * tq), NEG_BIG, F32)
        acc_ref[p] = jnp.zeros((LANES + SUM_ROWS, 2 * tq), F32)

    ones = jnp.ones((SUM_ROWS, tk), BF16)

    def step(j, masked):
        start = pl.multiple_of(j * tk, tk)
        scores = []
        for p in range(n_pairs):
            scores.append(_dot(ka_ref[p, pl.ds(start, tk), :], qs_ref[p]))
        probs, alphas = [], []
        for p in range(n_pairs):
            s = scores[p]
            if masked:
                s = jnp.concatenate([jnp.where(diag_ok, s[:, 0:tq], NEG_BIG),
                                     jnp.where(diag_ok, s[:, tq:2 * tq], NEG_BIG)], axis=1)
            m_prev = m_ref[p]
            m_new = jnp.maximum(m_prev, jnp.max(s, axis=0, keepdims=True))
            alpha = jnp.exp2(m_prev - m_new)
            pr = jnp.exp2(s - m_new)
            m_ref[p] = m_new
            probs.append(pr.astype(BF16))
            alphas.append(alpha)
        for p in range(n_pairs):
            vtb = jnp.concatenate([vt_ref[p * LANES:(p + 1) * LANES, pl.ds(start, tk)], ones], axis=0)
            acc_ref[p] = alphas[p] * acc_ref[p] + _dot(vtb, probs[p])

    def body(j, carry):
        step(j, False)
        return carry

    lax.fori_loop(0, qi, body, 0)
    step(qi, True)

    for p in range(n_pairs):
        o = acc_ref[p, 0:LANES, :] * (1.0 / acc_ref[p, LANES:LANES + 1, :])
        o = jnp.where(low, o[:, 0:tq], o[:, tq:2 * tq])
        o_ref[0, :, p * LANES:(p + 1) * LANES] = o.T.astype(o_ref.dtype)


def _fox_attn(qt, k3, vt, c4):
    b, t, d = k3.shape
    n_q = t // ATT_TILE
    width = ATT_PAIRS * LANES
    row_state = pltpu.VMEM((ATT_PAIRS, 1, 2 * ATT_TILE), F32)
    return pl.pallas_call(
        _fox_attn_kernel,
        grid=(b, d // width, t // ATT_TILE),
        in_specs=[
            pl.BlockSpec((width, ATT_TILE), lambda bi, g, qi: (g, bi * n_q + qi)),
            pl.BlockSpec((1, t, width), lambda bi, g, qi: (bi, 0, g)),
            pl.BlockSpec((width, t), lambda bi, g, qi: (g, bi)),
            pl.BlockSpec((1, 1, t, 2 * ATT_PAIRS), lambda bi, g, qi: (bi, g, 0, 0)),
        ],
        out_specs=pl.BlockSpec((1, ATT_TILE, width), lambda bi, g, qi: (bi, qi, g)),
        out_shape=jax.ShapeDtypeStruct((b, t, d), BF16),
        scratch_shapes=[pltpu.VMEM((ATT_PAIRS, t, 2 * LANES), BF16),
                        pltpu.VMEM((ATT_PAIRS, 2 * LANES, 2 * ATT_TILE), BF16), row_state,
                        pltpu.VMEM((ATT_PAIRS, LANES + SUM_ROWS, 2 * ATT_TILE), F32)],
        compiler_params=_cparams(("parallel", "parallel", "arbitrary")),
        name="fox_attn",
    )(qt, k3, vt, c4)


def _ffn_kernel(x_ref, g_in_ref, g_out_ref, wup_ref, cw_ref, wdn_ref, o_ref,
                h_ref, ext_ref, prev_ref, act_ref, acc_ref):
    tm = x_ref.shape[1]
    halo = SUBLANES
    fc = FF_CHUNK
    d_ff = wdn_ref.shape[0]
    n_chunks = d_ff // fc

    @pl.when(pl.program_id(1) == 0)
    def _():
        prev_ref[...] = jnp.zeros_like(prev_ref)

    x = x_ref[0]
    h_ref[...] = _rms_rows(x, g_in_ref[...]).astype(BF16)
    acc_ref[...] = jnp.zeros_like(acc_ref)

    def issue_up(j):
        ext = ext_ref.at[j % 2]
        ext[0:halo, :] = prev_ref[j]
        for half in range(2):
            w0 = half * d_ff + j * fc
            ext[halo:halo + tm, half * fc:(half + 1) * fc] = _dot(h_ref[...], wup_ref[:, w0:w0 + fc])

    grp = FFN_DOWN_GROUP
    n_groups = -(-n_chunks // grp)

    def issue_down(i):
        width = min(grp, n_chunks - grp * i) * fc
        r0 = grp * i * fc
        acc_ref[...] += _dot(act_ref[i % 2, :, 0:width], wdn_ref[r0:r0 + width, :])

    issue_up(0)
    for j in range(n_chunks):
        if j + 1 < n_chunks:
            issue_up(j + 1)
        if j >= grp and j % grp == 0:
            issue_down(j // grp - 1)
        ext = ext_ref.at[j % 2]
        prev_ref[j] = ext[tm:tm + halo, :]
        cw = jnp.concatenate([cw_ref[:, j * fc:(j + 1) * fc],
                              cw_ref[:, d_ff + j * fc:d_ff + (j + 1) * fc]], axis=1)
        for r0 in range(0, tm, FFN_ROWS):
            base = halo + r0
            conv = cw[FFN_CONV:FFN_CONV + 1] + cw[FFN_CONV - 1:FFN_CONV] * ext[base:base + FFN_ROWS, :]
            for s in range(1, FFN_CONV):
                conv = conv + cw[FFN_CONV - 1 - s:FFN_CONV - s] * ext[base - s:base - s + FFN_ROWS, :]
            act_ref[(j // grp) % 2, r0:r0 + FFN_ROWS, (j % grp) * fc:(j % grp + 1) * fc] = (
                _gelu_tanh_x2(conv[:, :fc]) * conv[:, fc:]).astype(BF16)
    for i in range((n_chunks - 1) // grp, n_groups):
        issue_down(i)
    o_ref[0] = x + _rms_rows(acc_ref[...], g_out_ref[...])


def _ffn(x3, g_in, g_out, w_up, taps, w_down):
    b, t, d = x3.shape
    n_chunks = w_down.shape[0] // FF_CHUNK
    fc2 = 2 * FF_CHUNK
    fixed2 = lambda bi, ti: (0, 0)
    return pl.pallas_call(
        _ffn_kernel,
        grid=(b, t // FFN_TILE),
        in_specs=[
            pl.BlockSpec((1, FFN_TILE, d), lambda bi, ti: (bi, ti, 0)),
            pl.BlockSpec((1, d), fixed2),
            pl.BlockSpec((1, d), fixed2),
            pl.BlockSpec(w_up.shape, fixed2, pipeline_mode=pl.Buffered(1)),
            pl.BlockSpec(taps.shape, fixed2, pipeline_mode=pl.Buffered(1)),
            pl.BlockSpec(w_down.shape, fixed2, pipeline_mode=pl.Buffered(1)),
        ],
        out_specs=pl.BlockSpec((1, FFN_TILE, d), lambda bi, ti: (bi, ti, 0)),
        out_shape=jax.ShapeDtypeStruct((b, t, d), F32),
        scratch_shapes=[
            pltpu.VMEM((FFN_TILE, d), BF16),
            pltpu.VMEM((2, FFN_TILE + SUBLANES, fc2), F32),
            pltpu.VMEM((n_chunks, SUBLANES, fc2), F32),
            pltpu.VMEM((2, FFN_TILE, FFN_DOWN_GROUP * FF_CHUNK), BF16),
            pltpu.VMEM((FFN_TILE, d), F32),
        ],
        compiler_params=_cparams(("parallel", "arbitrary")),
        name="conv_ffn",
    )(x3, g_in, g_out, w_up, taps, w_down)


def _ffn_taps(conv_w, conv_b):
    taps = jnp.concatenate([conv_w, conv_b[None, :]], axis=0)
    taps = jnp.concatenate([taps[:, :D_FF], 0.5 * taps[:, D_FF:]], axis=1)
    return jnp.pad(taps, ((0, SUBLANES - taps.shape[0]), (0, 0)))


def kernel(x, norm_gains, even_w_in, hgrn_lb_logits, hgrn_norm, rg_conv_w, rg_conv_b, rg_wa, rg_ba,
           rg_wx, rg_bx, rg_lambda, even_w_out, odd_w_in, fox_f_bias, odd_w_out,
           ffn_w_up, ffn_conv_w, ffn_conv_b, ffn_w_down):
    b, t, d = x.shape
    n = b * t
    x2 = x.reshape(n, d)
    for l in range(DEPTH):
        g = norm_gains[l]
        if l % 2 == 0:
            e = l // 2
            w_in = even_w_in[e]
            a2 = 2 * A_WIDTH
            w_main = jnp.concatenate([w_in[:, :A_WIDTH], w_in[:, a2:]], axis=1).astype(BF16)
            w_gates = jnp.concatenate(
                [_block_diag_dense(rg_wa[e]), _block_diag_dense(rg_wx[e])], axis=1).astype(BF16)
            b_gates = jnp.concatenate([rg_ba[e], rg_bx[e]])[None, :]
            x2 = _even_mixer(x2.reshape(b, t, d), g[0:1], w_in[:, A_WIDTH:a2].astype(BF16), w_main,
                             hgrn_lb_logits, hgrn_norm[e][None, :],
                             rg_conv_w[e], rg_conv_b[e][None, :], w_gates, b_gates,
                             rg_lambda[e][None, :], even_w_out[e].astype(BF16), g[1:2], l).reshape(n, d)
        else:
            o = l // 2
            w_in = odd_w_in[o]
            wqt = w_in[:, :d].T.astype(BF16)
            wk = w_in[:, d:2 * d].astype(BF16)
            wvt = w_in[:, 2 * d:3 * d].T.astype(BF16)
            wf = jnp.pad(w_in[:, 3 * d:], ((0, 0), (0, LANES - C_HEADS))).astype(BF16)
            qt, k, vt, f = _odd_inproj(x2, g[0:1], wqt, wk, wvt, wf)
            f_tb = f.reshape(b, t, LANES)[:, :, :C_HEADS].transpose(1, 0, 2).reshape(t, b * C_HEADS)
            bias_row = jnp.tile(fox_f_bias[o], b)[None, :]
            c_tb = _fox_cumsum(f_tb, bias_row)
            group = 2 * ATT_PAIRS
            c4 = c_tb.reshape(t, b, C_HEADS // group, group).transpose(1, 2, 0, 3)
            att = _fox_attn(qt, k.reshape(b, t, d), vt, c4)
            x2 = _outproj(att.reshape(n, d), odd_w_out[o].astype(BF16), x2, g[1:2])
        x2 = _ffn(x2.reshape(b, t, d), g[2:3], g[3:4], ffn_w_up[l].astype(BF16),
                  _ffn_taps(ffn_conv_w[l], ffn_conv_b[l]), ffn_w_down[l].astype(BF16)).reshape(n, d)
    return x2.reshape(b, t, d)
```

```python
import functools

import jax
import jax.numpy as jnp
from jax import lax
from jax.experimental import pallas as pl
from jax.experimental.pallas import tpu as pltpu

F32 = jnp.float32
BF16 = jnp.bfloat16

D_MODEL = 1024
DEPTH = 2
A_HEADS = 4
A_DIM = 128
A_WIDTH = A_HEADS * A_DIM
HGRN_CHUNK = 64
B_WIDTH = D_MODEL - A_WIDTH
B_CONV = 4
RG_C = 8.0
C_HEADS = 16
C_HEAD_DIM = D_MODEL // C_HEADS
D_FF = 2816
FFN_CONV = 3
EPS = 1e-6

LANES = 128
SUBLANES = 8
VMEM_LIMIT = 56 * 1024 * 1024

ROW_TILE = 1024
SEQ_TILE = 512
MIX_SUB = 256
RG_ROWS = 64
FFN_TILE = 512
FF_CHUNK = 256
FFN_DOWN_GROUP = 2
FFN_ROWS = 128
ATT_TILE = 256
ATT_PAIRS = 8
NEG_BIG = -1e30

LOG2E = 1.4426950408889634
C_PARTS = 3
SUM_ROWS = 16
GELU_C0 = 0.7978845608028654
GELU_C1 = 0.044715


def _cparams(semantics):
    return pltpu.CompilerParams(dimension_semantics=semantics, vmem_limit_bytes=VMEM_LIMIT)


def _rms_rows(x, gain):
    ms = jnp.mean(x * x, axis=-1, keepdims=True)
    return x * lax.rsqrt(ms + EPS) * gain


def _gelu_tanh_x2(x):
    inner = x * (GELU_C0 + (GELU_C0 * GELU_C1) * (x * x))
    return x * (1.0 + jnp.tanh(inner))


def _gelu_tanh(x):
    return 0.5 * _gelu_tanh_x2(x)


def _sigmoid(x):
    return 1.0 / (1.0 + jnp.exp(-x))


def _dot(a, b):
    return jnp.dot(a, b, preferred_element_type=F32)


def _dot_nt(a, b):
    return lax.dot_general(a, b, (((1,), (1,)), ((), ())), preferred_element_type=F32)


def _dot_tn(a, b):
    return lax.dot_general(a, b, (((0,), (0,)), ((), ())), preferred_element_type=F32)


def _interleave(*streams):
    live = list(streams)
    while live:
        for stream in list(live):
            if next(stream, _DONE) is _DONE:
                live.remove(stream)


_DONE = object()


def _split_bf16(x):
    hi = x.astype(BF16)
    lo = (x - hi.astype(F32)).astype(BF16)
    return hi, lo


def _odd_inproj_kernel(x_ref, g_ref, wqt_ref, wk_ref, wvt_ref, wf_ref, qt_ref, k_ref, vt_ref, f_ref):
    h = _rms_rows(x_ref[...], g_ref[...]).astype(BF16)
    qt_ref[...] = (_dot_nt(wqt_ref[...], h) * (LOG2E * C_HEAD_DIM ** -0.5)).astype(BF16)
    k_ref[...] = _dot(h, wk_ref[...]).astype(BF16)
    vt_ref[...] = _dot_nt(wvt_ref[...], h).astype(BF16)
    f_ref[...] = _dot(h, wf_ref[...])


def _odd_inproj(x2, gain, wqt_bf16, wk_bf16, wvt_bf16, wf_bf16):
    n, d = x2.shape
    row = lambda i: (i, 0)
    col = lambda i: (0, i)
    fixed = lambda i: (0, 0)
    return pl.pallas_call(
        _odd_inproj_kernel,
        grid=(n // ROW_TILE,),
        in_specs=[
            pl.BlockSpec((ROW_TILE, d), row),
            pl.BlockSpec((1, d), fixed),
            pl.BlockSpec((d, d), fixed),
            pl.BlockSpec((d, d), fixed),
            pl.BlockSpec((d, d), fixed),
            pl.BlockSpec((d, LANES), fixed),
        ],
        out_specs=[
            pl.BlockSpec((d, ROW_TILE), col),
            pl.BlockSpec((ROW_TILE, d), row),
            pl.BlockSpec((d, ROW_TILE), col),
            pl.BlockSpec((ROW_TILE, LANES), row),
        ],
        out_shape=[
            jax.ShapeDtypeStruct((d, n), BF16),
            jax.ShapeDtypeStruct((n, d), BF16),
            jax.ShapeDtypeStruct((d, n), BF16),
            jax.ShapeDtypeStruct((n, LANES), F32),
        ],
        compiler_params=_cparams(("parallel",)),
        name="odd_inproj",
    )(x2, gain, wqt_bf16, wk_bf16, wvt_bf16, wf_bf16)


def _hgrn_body(q, f_logit, v, g, lbz_ref, gain_ref, mix_ref, st_ref, *, layer):
    tt = q.shape[0]

    z = lbz_ref[...]
    ez = jnp.exp(z - jnp.max(z, axis=0, keepdims=True))
    lb = jnp.sum(ez[:layer + 1], axis=0, keepdims=True) / jnp.sum(ez, axis=0, keepdims=True)

    forget = lb + (1.0 - lb) * _sigmoid(f_logit)
    logf = jnp.log(forget)

    r = lax.broadcasted_iota(jnp.int32, (tt, tt), 0)
    c = lax.broadcasted_iota(jnp.int32, (tt, tt), 1)
    shift = HGRN_CHUNK.bit_length() - 1
    same_chunk = lax.shift_right_logical(r, shift) == lax.shift_right_logical(c, shift)
    tri = jnp.where((c <= r) & same_chunk, 1.0, 0.0).astype(BF16)
    hi, lo = _split_bf16(logf)
    bcum = _dot(tri, hi) + _dot(tri, lo)

    qs = q.astype(F32)
    qs = qs * _sigmoid(qs)
    kk = 1.0 - forget
    gate = _sigmoid(g.astype(F32))
    gain = gain_ref[...]

    cr = lax.broadcasted_iota(jnp.int32, (HGRN_CHUNK, HGRN_CHUNK), 0)
    cc = lax.broadcasted_iota(jnp.int32, (HGRN_CHUNK, HGRN_CHUNK), 1)
    causal = cc <= cr
    yield

    for ch in range(tt // HGRN_CHUNK):
        rows = slice(ch * HGRN_CHUNK, (ch + 1) * HGRN_CHUNK)
        bc = bcum[rows]
        b_last = bc[HGRN_CHUNK - 1:HGRN_CHUNK]
        q_dec = (qs[rows] * jnp.exp(bc)).astype(BF16)
        k_intra = (kk[rows] * jnp.exp(-bc)).astype(BF16)
        k_upd = (kk[rows] * jnp.exp(b_last - bc)).astype(BF16)
        dec = jnp.exp(b_last)
        v_bf = v[rows]
        outs = []
        for hd in range(A_HEADS):
            cols = slice(hd * A_DIM, (hd + 1) * A_DIM)
            st = st_ref[hd]
            scores = jnp.where(causal, _dot_nt(q_dec[:, cols], k_intra[:, cols]), 0.0)
            o = _dot(scores.astype(BF16), v_bf[:, cols]) + _dot_nt(q_dec[:, cols], st.astype(BF16))
            st_ref[hd] = dec[:, cols] * st + _dot_tn(v_bf[:, cols], k_upd[:, cols])
            o = o * lax.rsqrt(jnp.mean(o * o, axis=-1, keepdims=True) + EPS)
            outs.append(o)
        o_all = jnp.concatenate(outs, axis=-1)
        mix_ref[rows, 0:A_WIDTH] = (o_all * gain * gate[rows]).astype(mix_ref.dtype)
        yield


def _rglru_body(x_br, y_br, cw_ref, cb_ref, wg_ref, bg_ref, lam_ref, mix_ref,
                xext_ref, hprev_ref):
    tt = x_br.shape[0]
    halo = SUBLANES
    x = x_br.astype(F32)
    xext_ref[halo:halo + tt, :] = x
    cw = cw_ref[...]
    xf = cb_ref[...] + cw[B_CONV - 1:B_CONV] * x
    for j in range(1, B_CONV):
        xf = xf + cw[B_CONV - 1 - j:B_CONV - j] * xext_ref[halo - j:halo - j + tt, :]
    xext_ref[0:halo, :] = x[tt - halo:tt]

    gates = _dot(xf.astype(BF16), wg_ref[...]) + bg_ref[...]
    lam = lam_ref[...]
    softplus_neg = jnp.maximum(-lam, 0.0) + jnp.log(1.0 + jnp.exp(-jnp.abs(lam)))
    yield

    groups = RG_ROWS // SUBLANES
    row = lax.broadcasted_iota(jnp.int32, (groups, SUBLANES, B_WIDTH), 1)
    carry = hprev_ref[...]
    for r0 in range(0, tt, RG_ROWS):
        rows = slice(r0, r0 + RG_ROWS)
        r = _sigmoid(gates[rows, :B_WIDTH])
        i = _sigmoid(gates[rows, B_WIDTH:])
        log_a = (-RG_C) * r * softplus_neg
        a = jnp.exp(log_a)
        u = jnp.sqrt(1.0 - jnp.exp(2.0 * log_a)) * (i * xf[rows])
        a3 = a.reshape(groups, SUBLANES, B_WIDTH)
        u3 = u.reshape(groups, SUBLANES, B_WIDTH)
        d = 1
        while d < SUBLANES:
            keep = row >= d
            a_sh = jnp.where(keep, pltpu.roll(a3, d, 1), 1.0)
            u_sh = jnp.where(keep, pltpu.roll(u3, d, 1), 0.0)
            u3 = u3 + a3 * u_sh
            a3 = a3 * a_sh
            d *= 2
        hs = []
        for gi in range(groups):
            hg = a3[gi] * carry + u3[gi]
            hs.append(hg)
            carry = hg[SUBLANES - 1:SUBLANES]
        h = jnp.concatenate(hs, axis=0)
        mix_ref[rows, A_WIDTH:D_MODEL] = (h * _gelu_tanh(y_br[rows].astype(F32))).astype(mix_ref.dtype)
        yield
    hprev_ref[...] = carry


def _even_mixer_kernel(x_ref, gin_ref, wf_ref, win_ref, lbz_ref, gain_ref,
                       cw_ref, cb_ref, wg_ref, bg_ref, lam_ref, wout_ref, gout_ref,
                       o_ref, pf_ref, proj_ref, mix_ref, st_ref, xext_ref, hprev_ref, *, layer):
    @pl.when(pl.program_id(1) == 0)
    def _():
        st_ref[...] = jnp.zeros_like(st_ref)
        xext_ref[0:SUBLANES, :] = jnp.zeros((SUBLANES, B_WIDTH), F32)
        hprev_ref[...] = jnp.zeros_like(hprev_ref)

    gin = gin_ref[...]
    n_sub = x_ref.shape[1] // MIX_SUB

    def project(s):
        h = _rms_rows(x_ref[0, s * MIX_SUB:(s + 1) * MIX_SUB, :], gin).astype(BF16)
        pf_ref[s] = _dot(h, wf_ref[...])
        yield
        for blk in range(proj_ref.shape[1]):
            proj_ref[s, blk] = _dot(h, win_ref[:, blk * A_WIDTH:(blk + 1) * A_WIDTH]).astype(BF16)
            yield

    _interleave(project(0))
    for s in range(n_sub):
        f_logit = pf_ref[s]
        q, v, g, x_br, y_br = [proj_ref[s, blk] for blk in range(proj_ref.shape[1])]
        rows = slice(s * MIX_SUB, (s + 1) * MIX_SUB)
        mix_sub = mix_ref.at[rows]
        streams = [
            _hgrn_body(q, f_logit, v, g, lbz_ref, gain_ref, mix_sub, st_ref, layer=layer),
            _rglru_body(x_br, y_br, cw_ref, cb_ref, wg_ref, bg_ref, lam_ref, mix_sub, xext_ref, hprev_ref)]
        if s + 1 < n_sub:
            streams.append(project(s + 1))
        _interleave(*streams)
        mixed = _dot(mix_ref[rows, :], wout_ref[...])
        o_ref[0, rows, :] = x_ref[0, rows, :] + _rms_rows(mixed, gout_ref[...])


def _even_mixer(x3, g_in, w_f, w_main, lb_logits, norm_gain, conv_w, conv_b, w_gates, b_gates,
                lam, w_out, g_out, layer):
    b, t, d = x3.shape
    fixed = lambda bi, ti: (0, 0)
    full = lambda a: pl.BlockSpec(a.shape, fixed)
    resident = lambda a: pl.BlockSpec(a.shape, fixed, pipeline_mode=pl.Buffered(1))
    row_tile = pl.BlockSpec((1, SEQ_TILE, d), lambda bi, ti: (bi, ti, 0))
    return pl.pallas_call(
        functools.partial(_even_mixer_kernel, layer=layer),
        grid=(b, t // SEQ_TILE),
        in_specs=[row_tile, full(g_in), resident(w_f), resident(w_main),
                  full(lb_logits), full(norm_gain), full(conv_w), full(conv_b),
                  resident(w_gates), full(b_gates), full(lam), resident(w_out), full(g_out)],
        out_specs=row_tile,
        out_shape=jax.ShapeDtypeStruct((b, t, d), F32),
        scratch_shapes=[pltpu.VMEM((SEQ_TILE // MIX_SUB, MIX_SUB, A_WIDTH), F32),
                        pltpu.VMEM((SEQ_TILE // MIX_SUB, w_main.shape[1] // A_WIDTH, MIX_SUB, A_WIDTH), BF16),
                        pltpu.VMEM((SEQ_TILE, d), BF16),
                        pltpu.VMEM((A_HEADS, A_DIM, A_DIM), F32),
                        pltpu.VMEM((SEQ_TILE + SUBLANES, B_WIDTH), F32),
                        pltpu.VMEM((1, B_WIDTH), F32)],
        compiler_params=_cparams(("parallel", "arbitrary")),
        name="even_mixer",
    )(x3, g_in, w_f, w_main, lb_logits, norm_gain, conv_w, conv_b, w_gates, b_gates, lam,
      w_out, g_out)


def _block_diag_dense(w):
    nb, bd, _ = w.shape
    eye = jnp.eye(nb, dtype=w.dtype)
    return jnp.einsum('nij,nm->nimj', w, eye).reshape(nb * bd, nb * bd)


def _fox_cumsum_kernel(f_ref, b_ref, o_ref, *, blk):
    t, w = f_ref.shape
    z = f_ref[...] + b_ref[...]
    logsig = jnp.minimum(z, 0.0) - jnp.log(1.0 + jnp.exp(-jnp.abs(z)))
    r = lax.broadcasted_iota(jnp.int32, (blk, blk), 0)
    c = lax.broadcasted_iota(jnp.int32, (blk, blk), 1)
    tri = jnp.where(c <= r, 1.0, 0.0).astype(BF16)
    carry = jnp.zeros((1, w), F32)
    for s in range(0, t, blk):
        x = logsig[s:s + blk]
        hi = x.astype(BF16)
        r1 = x - hi.astype(F32)
        mid = r1.astype(BF16)
        lo = (r1 - mid.astype(F32)).astype(BF16)
        cs = _dot(tri, hi) + _dot(tri, mid) + _dot(tri, lo) + carry
        o_ref[s:s + blk, :] = cs * (-LOG2E)
        carry = cs[blk - 1:blk]


def _fox_cumsum(f_tb, bias_row):
    t, w = f_tb.shape
    return pl.pallas_call(
        functools.partial(_fox_cumsum_kernel, blk=256),
        out_shape=jax.ShapeDtypeStruct((t, w), F32),
        compiler_params=pltpu.CompilerParams(vmem_limit_bytes=VMEM_LIMIT),
        name="fox_cumsum",
    )(f_tb, bias_row)


def _fox_attn_kernel(q_ref, k_ref, vt_ref, c_ref, wout_ref, x_ref, gout_ref, o_ref,
                     ka_ref, qs_ref, m_ref, acc_ref, att_ref):
    tq = q_ref.shape[1]
    tk = tq
    n_pairs = q_ref.shape[0] // LANES
    qi = pl.program_id(2)
    dim = lax.broadcasted_iota(jnp.int32, (LANES, tq), 0)
    low = dim < C_HEAD_DIM
    key_row = lax.broadcasted_iota(jnp.int32, (tk, tq), 0)
    qry_col = lax.broadcasted_iota(jnp.int32, (tk, tq), 1)
    diag_ok = key_row <= qry_col

    @pl.when(qi == 0)
    def _():
        rest = c_ref[0, 0]
        n_heads = rest.shape[1]
        pieces = []
        for _ in range(C_PARTS):
            part = rest.astype(BF16)
            pieces.append(part)
            rest = rest - part.astype(F32)
        c_parts = jnp.concatenate(pieces, axis=1)
        src = lax.broadcasted_iota(jnp.int32, (C_PARTS * n_heads, LANES), 0)
        dst = lax.broadcasted_iota(jnp.int32, (C_PARTS * n_heads, LANES), 1)
        for p in range(n_pairs):
            pick = None
            for hd in range(2):
                for part in range(C_PARTS):
                    hit = (src == part * n_heads + 2 * p + hd) & (dst == hd * C_PARTS + part)
                    pick = hit if pick is None else (pick | hit)
            place = jnp.where(pick, 1.0, 0.0).astype(BF16)
            ka_ref[p, :, 0:LANES] = k_ref[0, :, p * LANES:(p + 1) * LANES]
            ka_ref[p, :, LANES:2 * LANES] = _dot(c_parts, place).astype(BF16)

    for p in range(n_pairs):
        q2t = q_ref[p * LANES:(p + 1) * LANES, :]
        zero = jnp.zeros_like(q2t)
        qs_ref[p, 0:LANES, 0:tq] = jnp.where(low, q2t, zero)
        qs_ref[p, 0:LANES, tq:2 * tq] = jnp.where(low, zero, q2t)
        qs_ref[p, LANES:2 * LANES, 0:tq] = jnp.where(dim < C_PARTS, 1.0, 0.0).astype(BF16)
        qs_ref[p, LANES:2 * LANES, tq:2 * tq] = jnp.where(
            (dim >= C_PARTS) & (dim < 2 * C_PARTS), 1.0, 0.0).astype(BF16)
        m_ref[p] = jnp.full((1, 2 * tq), NEG_BIG, F32)
        acc_ref[p] = jnp.zeros((LANES + SUM_ROWS, 2 * tq), F32)

    ones = jnp.ones((SUM_ROWS, tk), BF16)

    def step(j, masked):
        start = pl.multiple_of(j * tk, tk)
        scores = []
        for p in range(n_pairs):
            scores.append(_dot(ka_ref[p, pl.ds(start, tk), :], qs_ref[p]))
        probs, alphas = [], []
        for p in range(n_pairs):
            s = scores[p]
            if masked:
                s = jnp.concatenate([jnp.where(diag_ok, s[:, 0:tq], NEG_BIG),
                                     jnp.where(diag_ok, s[:, tq:2 * tq], NEG_BIG)], axis=1)
            m_prev = m_ref[p]
            m_new = jnp.maximum(m_prev, jnp.max(s, axis=0, keepdims=True))
            alpha = jnp.exp2(m_prev - m_new)
            pr = jnp.exp2(s - m_new)
            m_ref[p] = m_new
            probs.append(pr.astype(BF16))
            alphas.append(alpha)
        for p in range(n_pairs):
            vtb = jnp.concatenate([vt_ref[p * LANES:(p + 1) * LANES, pl.ds(start, tk)], ones], axis=0)
            acc_ref[p] = alphas[p] * acc_ref[p] + _dot(vtb, probs[p])

    def body(j, carry):
        step(j, False)
        return carry

    lax.fori_loop(0, qi, body, 0)
    step(qi, True)

    for p in range(n_pairs):
        o = acc_ref[p, 0:LANES, :] * (1.0 / acc_ref[p, LANES:LANES + 1, :])
        o = jnp.where(low, o[:, 0:tq], o[:, tq:2 * tq])
        att_ref[:, p * LANES:(p + 1) * LANES] = o.T.astype(att_ref.dtype)

    mixed = _dot(att_ref[...], wout_ref[...])
    o_ref[0] = x_ref[0] + _rms_rows(mixed, gout_ref[...])


def _fox_attn(qt, k3, vt, c4, w_out, x3, g_out):
    b, t, d = k3.shape
    n_q = t // ATT_TILE
    width = ATT_PAIRS * LANES
    assert width == d, "the fused out-projection needs all heads in one grid step"
    row_state = pltpu.VMEM((ATT_PAIRS, 1, 2 * ATT_TILE), F32)
    x_tile = pl.BlockSpec((1, ATT_TILE, d), lambda bi, g, qi: (bi, qi, 0))
    return pl.pallas_call(
        _fox_attn_kernel,
        grid=(b, d // width, t // ATT_TILE),
        in_specs=[
            pl.BlockSpec((width, ATT_TILE), lambda bi, g, qi: (g, bi * n_q + qi)),
            pl.BlockSpec((1, t, width), lambda bi, g, qi: (bi, 0, g)),
            pl.BlockSpec((width, t), lambda bi, g, qi: (g, bi)),
            pl.BlockSpec((1, 1, t, 2 * ATT_PAIRS), lambda bi, g, qi: (bi, g, 0, 0)),
            pl.BlockSpec(w_out.shape, lambda bi, g, qi: (0, 0), pipeline_mode=pl.Buffered(1)),
            x_tile,
            pl.BlockSpec((1, d), lambda bi, g, qi: (0, 0)),
        ],
        out_specs=x_tile,
        out_shape=jax.ShapeDtypeStruct((b, t, d), F32),
        scratch_shapes=[pltpu.VMEM((ATT_PAIRS, t, 2 * LANES), BF16),
                        pltpu.VMEM((ATT_PAIRS, 2 * LANES, 2 * ATT_TILE), BF16), row_state,
                        pltpu.VMEM((ATT_PAIRS, LANES + SUM_ROWS, 2 * ATT_TILE), F32),
                        pltpu.VMEM((ATT_TILE, d), BF16)],
        compiler_params=_cparams(("parallel", "parallel", "arbitrary")),
        name="fox_attn",
    )(qt, k3, vt, c4, w_out, x3, g_out)


def _ffn_kernel(x_ref, g_in_ref, g_out_ref, wup_ref, cw_ref, wdn_ref, o_ref,
                h_ref, ext_ref, prev_ref, act_ref, acc_ref):
    tm = x_ref.shape[1]
    halo = SUBLANES
    fc = FF_CHUNK
    d_ff = wdn_ref.shape[0]
    n_chunks = d_ff // fc

    @pl.when(pl.program_id(1) == 0)
    def _():
        prev_ref[...] = jnp.zeros_like(prev_ref)

    x = x_ref[0]
    h_ref[...] = _rms_rows(x, g_in_ref[...]).astype(BF16)
    acc_ref[...] = jnp.zeros_like(acc_ref)

    def issue_up(j):
        ext = ext_ref.at[j % 2]
        ext[0:halo, :] = prev_ref[j]
        for half in range(2):
            w0 = half * d_ff + j * fc
            ext[halo:halo + tm, half * fc:(half + 1) * fc] = _dot(h_ref[...], wup_ref[:, w0:w0 + fc])

    grp = FFN_DOWN_GROUP
    n_groups = -(-n_chunks // grp)

    def issue_down(i):
        width = min(grp, n_chunks - grp * i) * fc
        r0 = grp * i * fc
        acc_ref[...] += _dot(act_ref[i % 2, :, 0:width], wdn_ref[r0:r0 + width, :])

    issue_up(0)
    for j in range(n_chunks):
        if j + 1 < n_chunks:
            issue_up(j + 1)
        if j >= grp and j % grp == 0:
            issue_down(j // grp - 1)
        ext = ext_ref.at[j % 2]
        prev_ref[j] = ext[tm:tm + halo, :]
        cw = jnp.concatenate([cw_ref[:, j * fc:(j + 1) * fc],
                              cw_ref[:, d_ff + j * fc:d_ff + (j + 1) * fc]], axis=1)
        for r0 in range(0, tm, FFN_ROWS):
            base = halo + r0
            conv = cw[FFN_CONV:FFN_CONV + 1] + cw[FFN_CONV - 1:FFN_CONV] * ext[base:base + FFN_ROWS, :]
            for s in range(1, FFN_CONV):
                conv = conv + cw[FFN_CONV - 1 - s:FFN_CONV - s] * ext[base - s:base - s + FFN_ROWS, :]
            act_ref[(j // grp) % 2, r0:r0 + FFN_ROWS, (j % grp) * fc:(j % grp + 1) * fc] = (
                _gelu_tanh_x2(conv[:, :fc]) * conv[:, fc:]).astype(BF16)
    for i in range((n_chunks - 1) // grp, n_groups):
        issue_down(i)
    o_ref[0] = x + _rms_rows(acc_ref[...], g_out_ref[...])


def _ffn(x3, g_in, g_out, w_up, taps, w_down):
    b, t, d = x3.shape
    n_chunks = w_down.shape[0] // FF_CHUNK
    fc2 = 2 * FF_CHUNK
    fixed2 = lambda bi, ti: (0, 0)
    return pl.pallas_call(
        _ffn_kernel,
        grid=(b, t // FFN_TILE),
        in_specs=[
            pl.BlockSpec((1, FFN_TILE, d), lambda bi, ti: (bi, ti, 0)),
            pl.BlockSpec((1, d), fixed2),
            pl.BlockSpec((1, d), fixed2),
            pl.BlockSpec(w_up.shape, fixed2, pipeline_mode=pl.Buffered(1)),
            pl.BlockSpec(taps.shape, fixed2, pipeline_mode=pl.Buffered(1)),
            pl.BlockSpec(w_down.shape, fixed2, pipeline_mode=pl.Buffered(1)),
        ],
        out_specs=pl.BlockSpec((1, FFN_TILE, d), lambda bi, ti: (bi, ti, 0)),
        out_shape=jax.ShapeDtypeStruct((b, t, d), F32),
        scratch_shapes=[
            pltpu.VMEM((FFN_TILE, d), BF16),
            pltpu.VMEM((2, FFN_TILE + SUBLANES, fc2), F32),
            pltpu.VMEM((n_chunks, SUBLANES, fc2), F32),
            pltpu.VMEM((2, FFN_TILE, FFN_DOWN_GROUP * FF_CHUNK), BF16),
            pltpu.VMEM((FFN_TILE, d), F32),
        ],
        compiler_params=_cparams(("parallel", "arbitrary")),
        name="conv_ffn",
    )(x3, g_in, g_out, w_up, taps, w_down)


def _ffn_taps(conv_w, conv_b):
    taps = jnp.concatenate([conv_w, conv_b[None, :]], axis=0)
    taps = jnp.concatenate([taps[:, :D_FF], 0.5 * taps[:, D_FF:]], axis=1)
    return jnp.pad(taps, ((0, SUBLANES - taps.shape[0]), (0, 0)))


def kernel(x, norm_gains, even_w_in, hgrn_lb_logits, hgrn_norm, rg_conv_w, rg_conv_b, rg_wa, rg_ba,
           rg_wx, rg_bx, rg_lambda, even_w_out, odd_w_in, fox_f_bias, odd_w_out,
           ffn_w_up, ffn_conv_w, ffn_conv_b, ffn_w_down):
    b, t, d = x.shape
    n = b * t
    x2 = x.reshape(n, d)
    for l in range(DEPTH):
        g = norm_gains[l]
        if l % 2 == 0:
            e = l // 2
            w_in = even_w_in[e]
            a2 = 2 * A_WIDTH
            w_main = jnp.concatenate([w_in[:, :A_WIDTH], w_in[:, a2:]], axis=1).astype(BF16)
            w_gates = jnp.concatenate(
                [_block_diag_dense(rg_wa[e]), _block_diag_dense(rg_wx[e])], axis=1).astype(BF16)
            b_gates = jnp.concatenate([rg_ba[e], rg_bx[e]])[None, :]
            x2 = _even_mixer(x2.reshape(b, t, d), g[0:1], w_in[:, A_WIDTH:a2].astype(BF16), w_main,
                             hgrn_lb_logits, hgrn_norm[e][None, :],
                             rg_conv_w[e], rg_conv_b[e][None, :], w_gates, b_gates,
                             rg_lambda[e][None, :], even_w_out[e].astype(BF16), g[1:2], l).reshape(n, d)
        else:
            o = l // 2
            w_in = odd_w_in[o]
            wqt = w_in[:, :d].T.astype(BF16)
            wk = w_in[:, d:2 * d].astype(BF16)
            wvt = w_in[:, 2 * d:3 * d].T.astype(BF16)
            wf = jnp.pad(w_in[:, 3 * d:], ((0, 0), (0, LANES - C_HEADS))).astype(BF16)
            qt, k, vt, f = _odd_inproj(x2, g[0:1], wqt, wk, wvt, wf)
            f_tb = f.reshape(b, t, LANES)[:, :, :C_HEADS].transpose(1, 0, 2).reshape(t, b * C_HEADS)
            bias_row = jnp.tile(fox_f_bias[o], b)[None, :]
            c_tb = _fox_cumsum(f_tb, bias_row)
            group = 2 * ATT_PAIRS
            c4 = c_tb.reshape(t, b, C_HEADS // group, group).transpose(1, 2, 0, 3)
            x2 = _fox_attn(qt, k.reshape(b, t, d), vt, c4, odd_w_out[o].astype(BF16),
                           x2.reshape(b, t, d), g[1:2]).reshape(n, d)
        x2 = _ffn(x2.reshape(b, t, d), g[2:3], g[3:4], ffn_w_up[l].astype(BF16),
                  _ffn_taps(ffn_conv_w[l], ffn_conv_b[l]), ffn_w_down[l].astype(BF16)).reshape(n, d)
    return x2.reshape(b, t, d)
```
